```python
import jax, jax.numpy as jnp
from jax import lax
import numpy as np

D_MODEL = 1024
BATCH = 4
SEQ = 4096
DEPTH = 1
DEC_BATCH = 128
DEC_SEQ = 4
PAST_LEN = 2048
PAGE_SIZE = 128

N_HEADS_ATT = 8
HEAD_DIM = 64
D_ATT = N_HEADS_ATT * HEAD_DIM
D_CONV = D_MODEL - D_ATT
CONV_WIDTH = 31
Q_BLOCK = 128
D_IN = 3 * D_ATT + N_HEADS_ATT + 2 * D_CONV
PEER_HEADS = 8
PEER_N_KEYS = 128
PEER_N_EXPERTS = PEER_N_KEYS * PEER_N_KEYS
PEER_QUERY_DIM = 256
PEER_HALF = PEER_QUERY_DIM // 2
PEER_TOPK = 16
PEER_TOKEN_BLOCK = 128
EPS = 1e-6
FORGET_BIAS_INIT = 3.0

kernel_name = 'hymba_fox_conformer_peer_adaln_step'


def _rms(x, g):
    xf = x.astype(jnp.float32)
    y = xf * lax.rsqrt(jnp.mean(xf * xf, axis=-1, keepdims=True) + EPS)
    return (y * g.astype(jnp.float32)).astype(x.dtype)


def _adaln(c, w_ada, b_ada):
    mod = (c @ w_ada + b_ada)[:, None, :]
    return jnp.split(mod, 6, axis=-1)


def _modulate(h, shift, scale):
    return h * (1 + scale) + shift


def _in_proj(h, w_in, b_f, b_glu, q_norm_g, k_norm_g):
    n, t, _ = h.shape
    z = h @ w_in
    q, k, v, fg, glu = jnp.split(
        z, [D_ATT, 2 * D_ATT, 3 * D_ATT, 3 * D_ATT + N_HEADS_ATT], axis=-1)
    q = _rms(q.reshape(n, t, N_HEADS_ATT, HEAD_DIM), q_norm_g)
    k = _rms(k.reshape(n, t, N_HEADS_ATT, HEAD_DIM), k_norm_g)
    v = v.reshape(n, t, N_HEADS_ATT, HEAD_DIM)
    logf = jax.nn.log_sigmoid((fg + b_f).astype(jnp.float32))
    a, b = jnp.split(glu + b_glu, 2, axis=-1)
    u = a * jax.nn.sigmoid(b)
    return q, k, v, logf, u


def _attend(q, k, v, fq, fk, q_pos, k_pos):
    s = jnp.einsum('nqhd,nkhd->nhqk', q, k).astype(jnp.float32) * (HEAD_DIM ** -0.5)
    s = s + jnp.swapaxes(fq, 1, 2)[..., :, None] - jnp.swapaxes(fk, 1, 2)[..., None, :]
    mask = k_pos[None, :] <= q_pos[:, None]
    s = jnp.where(mask, s, -jnp.inf)
    p = jax.nn.softmax(s, axis=-1)
    return jnp.einsum('nhqk,nkhd->nqhd', p.astype(v.dtype), v)


def _prompt_attention(q, k, v, logf):
    n, t, h, d = q.shape
    nb = t // Q_BLOCK
    f_cum = jnp.cumsum(logf, axis=1)
    pos = jnp.arange(t)
    qb = jnp.moveaxis(q.reshape(n, nb, Q_BLOCK, h, d), 1, 0)
    fb = jnp.moveaxis(f_cum.reshape(n, nb, Q_BLOCK, h), 1, 0)
    pb = pos.reshape(nb, Q_BLOCK)
    out = lax.map(lambda a: _attend(a[0], k, v, a[1], f_cum, a[2], pos), (qb, fb, pb))
    return jnp.moveaxis(out, 0, 1).reshape(n, t, h * d)


def _sample_attention(q, k, v, logf, cache_k, cache_v, cache_logf, page_table):
    n, s = q.shape[:2]
    past = page_table.shape[1] * PAGE_SIZE
    kp = cache_k[page_table].reshape(n, past, N_HEADS_ATT, HEAD_DIM).astype(k.dtype)
    vp = cache_v[page_table].reshape(n, past, N_HEADS_ATT, HEAD_DIM).astype(v.dtype)
    lp = cache_logf[page_table].reshape(n, past, N_HEADS_ATT).astype(jnp.float32)
    k_all = jnp.concatenate([kp, k], axis=1)
    v_all = jnp.concatenate([vp, v], axis=1)
    f_cum = jnp.cumsum(jnp.concatenate([lp, logf], axis=1), axis=1)
    k_pos = jnp.arange(past + s)
    q_pos = past + jnp.arange(s)
    out = _attend(q, k_all, v_all, f_cum[:, past:], f_cum, q_pos, k_pos)
    return out.reshape(n, s, D_ATT)


def _conv_branch(u_ext, conv_w, conv_b, ln_g, ln_b):
    y = lax.conv_general_dilated(
        u_ext, conv_w[:, None, :].astype(u_ext.dtype), (1,), 'VALID',
        dimension_numbers=('NWC', 'WIO', 'NWC'), feature_group_count=D_CONV) + conv_b
    yf = y.astype(jnp.float32)
    mu = jnp.mean(yf, axis=-1, keepdims=True)
    var = jnp.mean(jnp.square(yf - mu), axis=-1, keepdims=True)
    yn = (yf - mu) * lax.rsqrt(var + EPS) * ln_g + ln_b
    return jax.nn.silu(yn).astype(u_ext.dtype)


def _peer(h, w_query, sub_keys, expert_u, expert_v):
    shape = h.shape
    hf = h.reshape(-1, D_MODEL)
    n = hf.shape[0]
    nb = -(-n // PEER_TOKEN_BLOCK)
    hf = jnp.pad(hf, ((0, nb * PEER_TOKEN_BLOCK - n), (0, 0))).reshape(nb, PEER_TOKEN_BLOCK, D_MODEL)

    def block(hb):
        q = (hb @ w_query).reshape(PEER_TOKEN_BLOCK, PEER_HEADS, 2, PEER_HALF)
        s = jnp.einsum('nhpd,hpkd->nhpk', q, sub_keys).astype(jnp.float32)
        s1, i1 = lax.top_k(s[:, :, 0], PEER_TOPK)
        s2, i2 = lax.top_k(s[:, :, 1], PEER_TOPK)
        cand = (s1[..., :, None] + s2[..., None, :]).reshape(PEER_TOKEN_BLOCK, PEER_HEADS, PEER_TOPK * PEER_TOPK)
        cidx = (i1[..., :, None] * PEER_N_KEYS + i2[..., None, :]).reshape(PEER_TOKEN_BLOCK, PEER_HEADS, PEER_TOPK * PEER_TOPK)
        top, sel = lax.top_k(cand, PEER_TOPK)
        eidx = jnp.take_along_axis(cidx, sel, axis=-1)
        g = jax.nn.softmax(top, axis=-1)
        act = jax.nn.gelu(jnp.einsum('nd,nhkd->nhk', hb, expert_u[eidx]).astype(jnp.float32), approximate=False)
        return jnp.einsum('nhk,nhkd->nd', (g * act).astype(hb.dtype), expert_v[eidx])

    out = lax.map(block, hf).reshape(nb * PEER_TOKEN_BLOCK, D_MODEL)[:n]
    return out.reshape(shape)


def _tail(x, attn_out, conv_out, gate1, shift2, scale2, gate2, w_o, norm2_g,
          w_query, sub_keys, expert_u, expert_v):
    x1 = x + gate1 * (jnp.concatenate([attn_out, conv_out], axis=-1) @ w_o)
    h2 = _modulate(_rms(x1, norm2_g), shift2, scale2)
    return x1 + gate2 * _peer(h2, w_query, sub_keys, expert_u, expert_v)


def setup_inputs(seed: int = 0) -> dict:
    key = jax.random.key(seed)
    ks = jax.random.split(key, 32)
    f32 = jnp.float32
    n_pages = PAST_LEN // PAGE_SIZE
    n_phys = (DEC_BATCH * n_pages * 5) // 4

    def nrm(k, shape, s):
        return s * jax.random.normal(k, shape, f32)

    page_table = jax.random.permutation(ks[0], n_phys)[: DEC_BATCH * n_pages]
    page_table = page_table.reshape(DEC_BATCH, n_pages).astype(jnp.int32)
    return {
        'x_prompt': nrm(ks[1], (BATCH, SEQ, D_MODEL), 1.0),
        'x_sample': nrm(ks[2], (DEC_BATCH, DEC_SEQ, D_MODEL), 1.0),
        'c_prompt': nrm(ks[3], (BATCH, D_MODEL), 1.0),
        'c_sample': nrm(ks[4], (DEC_BATCH, D_MODEL), 1.0),
        'cache_k': nrm(ks[5], (DEPTH, n_phys, PAGE_SIZE, N_HEADS_ATT, HEAD_DIM), 1.0),
        'cache_v': nrm(ks[6], (DEPTH, n_phys, PAGE_SIZE, N_HEADS_ATT, HEAD_DIM), 1.0),
        'cache_logf': jax.nn.log_sigmoid(FORGET_BIAS_INIT + nrm(ks[7], (DEPTH, n_phys, PAGE_SIZE, N_HEADS_ATT), 0.5)),
        'state_conv': nrm(ks[8], (DEPTH, DEC_BATCH, CONV_WIDTH - 1, D_CONV), 0.5),
        'page_table': page_table,
        'norm1_g': 1.0 + nrm(ks[9], (DEPTH, D_MODEL), 0.05),
        'norm2_g': 1.0 + nrm(ks[10], (DEPTH, D_MODEL), 0.05),
        'w_ada': nrm(ks[11], (DEPTH, D_MODEL, 6 * D_MODEL), 0.5 * D_MODEL ** -0.5),
        'b_ada': nrm(ks[12], (DEPTH, 6 * D_MODEL), 0.02),
        'w_in': nrm(ks[13], (DEPTH, D_MODEL, D_IN), D_MODEL ** -0.5),
        'b_f': FORGET_BIAS_INIT + nrm(ks[14], (DEPTH, N_HEADS_ATT), 0.5),
        'b_glu': nrm(ks[15], (DEPTH, 2 * D_CONV), 0.02),
        'q_norm_g': 1.0 + nrm(ks[16], (DEPTH, HEAD_DIM), 0.05),
        'k_norm_g': 1.0 + nrm(ks[17], (DEPTH, HEAD_DIM), 0.05),
        'conv_w': nrm(ks[18], (DEPTH, CONV_WIDTH, D_CONV), CONV_WIDTH ** -0.5),
        'conv_b': nrm(ks[19], (DEPTH, D_CONV), 0.02),
        'conv_ln_g': 1.0 + nrm(ks[20], (DEPTH, D_CONV), 0.05),
        'conv_ln_b': nrm(ks[21], (DEPTH, D_CONV), 0.02),
        'w_o': nrm(ks[22], (DEPTH, D_MODEL, D_MODEL), D_MODEL ** -0.5),
        'peer_w_query': nrm(ks[23], (DEPTH, D_MODEL, PEER_HEADS * PEER_QUERY_DIM), D_MODEL ** -0.5),
        'peer_sub_keys': nrm(ks[24], (DEPTH, PEER_HEADS, 2, PEER_N_KEYS, PEER_HALF), PEER_HALF ** -0.5),
        'peer_u': nrm(ks[25], (DEPTH, PEER_N_EXPERTS, D_MODEL), D_MODEL ** -0.5),
        'peer_v': nrm(ks[26], (DEPTH, PEER_N_EXPERTS, D_MODEL), PEER_HEADS ** -0.5),
    }


def reference(x_prompt, x_sample, c_prompt, c_sample, cache_k, cache_v, cache_logf,
              state_conv, page_table, norm1_g, norm2_g, w_ada, b_ada, w_in, b_f, b_glu,
              q_norm_g, k_norm_g, conv_w, conv_b, conv_ln_g, conv_ln_b, w_o,
              peer_w_query, peer_sub_keys, peer_u, peer_v):
    xp, xs = x_prompt, x_sample
    kps, vps, lps, cps = [], [], [], []
    kss, vss, lss, css = [], [], [], []
    for l in range(DEPTH):
        sh1, sc1, g1, sh2, sc2, g2 = _adaln(c_prompt, w_ada[l], b_ada[l])
        h = _modulate(_rms(xp, norm1_g[l]), sh1, sc1)
        q, k, v, logf, u = _in_proj(h, w_in[l], b_f[l], b_glu[l], q_norm_g[l], k_norm_g[l])
        a_out = _prompt_attention(q, k, v, logf)
        u_ext = jnp.concatenate([jnp.zeros((u.shape[0], CONV_WIDTH - 1, D_CONV), u.dtype), u], axis=1)
        c_out = _conv_branch(u_ext, conv_w[l], conv_b[l], conv_ln_g[l], conv_ln_b[l])
        xp = _tail(xp, a_out, c_out, g1, sh2, sc2, g2, w_o[l], norm2_g[l],
                   peer_w_query[l], peer_sub_keys[l], peer_u[l], peer_v[l])
        kps.append(k)
        vps.append(v)
        lps.append(logf)
        cps.append(u_ext[:, -(CONV_WIDTH - 1):])
        sh1, sc1, g1, sh2, sc2, g2 = _adaln(c_sample, w_ada[l], b_ada[l])
        h = _modulate(_rms(xs, norm1_g[l]), sh1, sc1)
        q, k, v, logf, u = _in_proj(h, w_in[l], b_f[l], b_glu[l], q_norm_g[l], k_norm_g[l])
        a_out = _sample_attention(q, k, v, logf, cache_k[l], cache_v[l], cache_logf[l], page_table)
        u_ext = jnp.concatenate([state_conv[l].astype(u.dtype), u], axis=1)
        c_out = _conv_branch(u_ext, conv_w[l], conv_b[l], conv_ln_g[l], conv_ln_b[l])
        xs = _tail(xs, a_out, c_out, g1, sh2, sc2, g2, w_o[l], norm2_g[l],
                   peer_w_query[l], peer_sub_keys[l], peer_u[l], peer_v[l])
        kss.append(k)
        vss.append(v)
        lss.append(logf)
        css.append(u_ext[:, -(CONV_WIDTH - 1):])
    return (xp, xs, jnp.stack(kps), jnp.stack(vps), jnp.stack(lps), jnp.stack(cps),
            jnp.stack(kss), jnp.stack(vss), jnp.stack(lss), jnp.stack(css))
```

```python
import functools

import jax
import jax.numpy as jnp
from jax import lax
from jax.experimental import pallas as pl
from jax.experimental.pallas import tpu as pltpu

F32 = jnp.float32
BF16 = jnp.bfloat16
I32 = jnp.int32

EPS = 1e-6
N_HEADS = 8
HEAD_DIM = 64
D_ATT = N_HEADS * HEAD_DIM
CONV_WIDTH = 31
PAGE_SIZE = 128
PEER_HEADS = 8
PEER_N_KEYS = 128
PEER_HALF = 128
PEER_TOPK = 16
PEER_SLOTS = PEER_HEADS * PEER_TOPK
LANES = 128
SUBLANES = 8
NEG = -1e30
VMEM_LIMIT = 56 * 1024 * 1024


def _cp(sem, vmem=VMEM_LIMIT):
    return pltpu.CompilerParams(dimension_semantics=sem, vmem_limit_bytes=vmem)


def _dot(a, b):
    return jnp.dot(a, b, preferred_element_type=F32)


def _dot_nt(a, b):
    return lax.dot_general(a, b, (((1,), (1,)), ((), ())), preferred_element_type=F32)


def _split2(a):
    hi = a.astype(BF16)
    lo = (a - hi.astype(F32)).astype(BF16)
    return hi, lo


def _split3(a):
    hi = a.astype(BF16)
    r = a - hi.astype(F32)
    mid = r.astype(BF16)
    lo = (r - mid.astype(F32)).astype(BF16)
    return hi, mid, lo


def _log_sigmoid(x):
    return jnp.minimum(x, 0.0) - jnp.log1p(jnp.exp(-jnp.abs(x)))


def _ada_kernel(c_ref, w_ref, b_ref, o_ref):
    ch, cl = _split2(c_ref[...])
    wh, wl = _split2(w_ref[...])
    o_ref[...] = _dot(ch, wh) + (_dot(ch, wl) + _dot(cl, wh)) + b_ref[...]


def _adaln(c, w_ada, b_ada):
    r, d = c.shape
    n = w_ada.shape[1]
    tn = 512
    return pl.pallas_call(
        _ada_kernel,
        grid=(n // tn,),
        in_specs=[pl.BlockSpec((r, d), lambda j: (0, 0)),
                  pl.BlockSpec((d, tn), lambda j: (0, j)),
                  pl.BlockSpec((1, tn), lambda j: (0, j))],
        out_specs=pl.BlockSpec((r, tn), lambda j: (0, j)),
        out_shape=jax.ShapeDtypeStruct((r, n), F32),
        compiler_params=_cp(("parallel",)),
        name="adaln",
    )(c, w_ada, b_ada.reshape(1, n))


def _inproj_kernel(x_ref, sh_ref, sc_ref, g1_ref, w_ref, wfh_ref, wfl_ref, bf_ref, bglu_ref,
                   qg_ref, kg_ref, ones_ref, tri_ref,
                   q_ref, k_ref, v_ref, kb_ref, vb_ref, lf_ref, fc_ref, u_ref, carry_ref):
    t = pl.program_id(1)

    @pl.when(t == 0)
    def _():
        carry_ref[...] = jnp.zeros_like(carry_ref)

    x = x_ref[...]
    ms = jnp.mean(x * x, axis=-1, keepdims=True)
    h = x * lax.rsqrt(ms + EPS) * g1_ref[...]
    h = h * (1.0 + sc_ref[...]) + sh_ref[...]
    hh, hl = _split2(h)
    z = _dot(hh, w_ref[...])
    fg = _dot(hh, wfh_ref[...]) + (_dot(hh, wfl_ref[...]) + _dot(hl, wfh_ref[...]))
    d = D_ATT
    q = z[:, 0:d]
    k = z[:, d:2 * d]
    v = z[:, 2 * d:3 * d]
    a = z[:, 3 * d:4 * d] + bglu_ref[:, 0:d]
    b = z[:, 4 * d:5 * d] + bglu_ref[:, d:2 * d]
    qs = _dot((q * q).astype(BF16), ones_ref[...])
    ks = _dot((k * k).astype(BF16), ones_ref[...])
    qn = q * lax.rsqrt(qs * (1.0 / HEAD_DIM) + EPS) * qg_ref[...]
    kn = k * lax.rsqrt(ks * (1.0 / HEAD_DIM) + EPS) * kg_ref[...]
    q_ref[...] = (qn * (HEAD_DIM ** -0.5)).astype(BF16)
    k_ref[...] = kn
    kb_ref[...] = kn.astype(BF16)
    v_ref[...] = v
    vb_ref[...] = v.astype(BF16)
    u_ref[...] = a * jax.nn.sigmoid(b)
    lf = _log_sigmoid(fg + bf_ref[...])
    lf_ref[...] = lf[:, 0:N_HEADS]
    l1, l2, l3 = _split3(lf)
    tri = tri_ref[...]
    fc = _dot(tri, l1) + (_dot(tri, l2) + _dot(tri, l3)) + carry_ref[...]
    fc_ref[...] = fc[:, 0:N_HEADS]
    tm = fc.shape[0]
    carry_ref[...] = fc[tm - 1:tm, :]


def _inproj(x, shift, scale, g1, wts, per_token, tm=256):
    bsz, t, d = x.shape
    w, wfh, wfl, bfp, bglu, qg, kg, ones_bd, tri = wts
    row = lambda b, i: (b, i, 0)
    const2 = lambda b, i: (0, 0)
    mod_spec = (pl.BlockSpec((None, tm, d), row) if per_token
                else pl.BlockSpec((None, 1, d), lambda b, i: (b, 0, 0)))
    full = lambda arr: pl.BlockSpec(arr.shape, const2)
    outs = [
        jax.ShapeDtypeStruct((bsz, t, D_ATT), BF16),
        jax.ShapeDtypeStruct((bsz, t, D_ATT), F32),
        jax.ShapeDtypeStruct((bsz, t, D_ATT), F32),
        jax.ShapeDtypeStruct((bsz, t, D_ATT), BF16),
        jax.ShapeDtypeStruct((bsz, t, D_ATT), BF16),
        jax.ShapeDtypeStruct((bsz, t, N_HEADS), F32),
        jax.ShapeDtypeStruct((bsz, t, N_HEADS), F32),
        jax.ShapeDtypeStruct((bsz, t, D_ATT), F32),
    ]
    ospec = lambda s: pl.BlockSpec((None, tm, s.shape[2]), row)
    return pl.pallas_call(
        _inproj_kernel,
        grid=(bsz, t // tm),
        in_specs=[pl.BlockSpec((None, tm, d), row), mod_spec, mod_spec, full(g1), full(w),
                  full(wfh), full(wfl), full(bfp), full(bglu), full(qg), full(kg),
                  full(ones_bd), full(tri)],
        out_specs=[ospec(s) for s in outs],
        out_shape=outs,
        scratch_shapes=[pltpu.VMEM((1, LANES), F32)],
        compiler_params=_cp(("parallel", "arbitrary")),
        name="inproj",
    )(x, shift, scale, g1, w, wfh, wfl, bfp, bglu, qg, kg, ones_bd, tri)


def _attn_kernel(q_ref, k_ref, v_ref, f_ref, o_ref, acc_ref, *, tq, tk):
    qi = pl.program_id(2)
    q = q_ref[...]
    lane = lax.broadcasted_iota(I32, (1, LANES), 1)
    zero = jnp.zeros_like(q)
    qm = [jnp.where(lane < HEAD_DIM, q, zero), jnp.where(lane >= HEAD_DIM, q, zero)]
    acc_ref[...] = jnp.zeros_like(acc_ref)
    rows = lax.broadcasted_iota(I32, (tq, tk), 0)
    cols = lax.broadcasted_iota(I32, (tq, tk), 1)

    def step(j, carry, diag_off=None):
        start = pl.multiple_of(j * tk, tk)
        kc = k_ref[pl.ds(start, tk), :]
        vc = v_ref[pl.ds(start, tk), :]
        fr = f_ref[j]
        new = []
        for hh in range(2):
            m, l = carry[2 * hh], carry[2 * hh + 1]
            s = _dot_nt(qm[hh], kc) - fr[hh:hh + 1, :]
            if diag_off is not None:
                s = jnp.where(cols + diag_off <= rows, s, NEG)
            m_new = jnp.maximum(m, jnp.max(s, axis=1, keepdims=True))
            alpha = jnp.exp(m - m_new)
            p = jnp.exp(s - m_new)
            l_new = alpha * l + jnp.sum(p, axis=1, keepdims=True)
            acc_ref[hh] = alpha * acc_ref[hh] + _dot(p.astype(BF16), vc)
            new += [m_new, l_new]
        return tuple(new)

    init = tuple(jnp.full((tq, 1), NEG, F32) if i % 2 == 0 else jnp.zeros((tq, 1), F32)
                 for i in range(4))
    n_diag = tq // tk
    carry = lax.fori_loop(0, qi * n_diag, step, init)
    for dd in range(n_diag):
        carry = step(qi * n_diag + dd, carry, diag_off=dd * tk)
    o0 = acc_ref[0] / carry[1]
    o1 = acc_ref[1] / carry[3]
    o_ref[...] = jnp.where(lane < HEAD_DIM, o0, o1).astype(o_ref.dtype)


def _prompt_attention(qb, kb, vb, fcum, tq=512, tk=256):
    bsz, t, _ = qb.shape
    n_pairs = N_HEADS // 2
    nk = t // tk
    f_rows = fcum.transpose(0, 2, 1).reshape(bsz, n_pairs, 2, nk, tk).transpose(0, 1, 3, 2, 4)
    return pl.pallas_call(
        functools.partial(_attn_kernel, tq=tq, tk=tk),
        grid=(bsz, n_pairs, t // tq),
        in_specs=[pl.BlockSpec((None, tq, LANES), lambda b, p, i: (b, i, p)),
                  pl.BlockSpec((None, t, LANES), lambda b, p, i: (b, 0, p)),
                  pl.BlockSpec((None, t, LANES), lambda b, p, i: (b, 0, p)),
                  pl.BlockSpec((None, None, nk, 2, tk), lambda b, p, i: (b, p, 0, 0, 0))],
        out_specs=pl.BlockSpec((None, tq, LANES), lambda b, p, i: (b, i, p)),
        out_shape=jax.ShapeDtypeStruct((bsz, t, D_ATT), BF16),
        scratch_shapes=[pltpu.VMEM((2, tq, LANES), F32)],
        compiler_params=_cp(("parallel", "parallel", "arbitrary")),
        name="prompt_attn",
    )(qb, kb, vb, f_rows)


def _sattn_kernel(pt_ref, q_ref, ck_ref, cv_ref, clf_ref, kn_ref, vn_ref, lfn_ref, triu_ref,
                  o_ref, m_ref, l_ref, acc_ref, fcar_ref, *, n_new):
    p = pl.program_id(1)
    n_pages = pl.num_programs(1)
    nq = n_new * N_HEADS
    lane = lax.broadcasted_iota(I32, (N_HEADS, D_ATT), 1)
    hrow = lax.broadcasted_iota(I32, (N_HEADS, D_ATT), 0)
    hmask = (lane // HEAD_DIM == hrow)

    @pl.when(p == 0)
    def _():
        m_ref[...] = jnp.full_like(m_ref, NEG)
        l_ref[...] = jnp.zeros_like(l_ref)
        acc_ref[...] = jnp.zeros_like(acc_ref)
        fcar_ref[...] = jnp.zeros_like(fcar_ref)

    q = q_ref[...]
    zq = jnp.zeros((N_HEADS, D_ATT), F32)
    qexp = jnp.concatenate(
        [jnp.where(hmask, jnp.broadcast_to(q[t:t + 1, :], (N_HEADS, D_ATT)), zq)
         for t in range(n_new)], axis=0).astype(BF16)

    def update(s, vmat):
        m = m_ref[...]
        m_new = jnp.maximum(m, jnp.max(s, axis=1, keepdims=True))
        alpha = jnp.exp(m - m_new)
        pr = jnp.exp(s - m_new)
        l_ref[...] = alpha * l_ref[...] + jnp.sum(pr, axis=1, keepdims=True)
        acc_ref[...] = alpha * acc_ref[...] + _dot(pr.astype(BF16), vmat)
        m_ref[...] = m_new

    l1, l2, l3 = _split3(clf_ref[...])
    tri = triu_ref[...]
    fpage = _dot(l1, tri) + (_dot(l2, tri) + _dot(l3, tri)) + fcar_ref[...]
    fcar_ref[...] = jnp.broadcast_to(fpage[:, PAGE_SIZE - 1:PAGE_SIZE], fcar_ref.shape)
    s = _dot_nt(qexp, ck_ref[...].astype(BF16))
    s = s - jnp.concatenate([fpage] * n_new, axis=0)
    update(s, cv_ref[...].astype(BF16))

    @pl.when(p == n_pages - 1)
    def _():
        ftot = fcar_ref[...][:, 0:1]
        lfn = lfn_ref[...]
        cols = []
        run = ftot
        for t in range(n_new):
            run = run + lfn[:, t:t + 1]
            cols.append(run)
        pad = SUBLANES - n_new
        fnew = jnp.concatenate(cols + [jnp.zeros((N_HEADS, pad), F32)], axis=1)
        zk = jnp.zeros((pad, D_ATT), BF16)
        kn = jnp.concatenate([kn_ref[...].astype(BF16), zk], axis=0)
        vn = jnp.concatenate([vn_ref[...].astype(BF16), zk], axis=0)
        s2 = _dot_nt(qexp, kn) - jnp.concatenate([fnew] * n_new, axis=0)
        r = lax.broadcasted_iota(I32, (nq, SUBLANES), 0) // N_HEADS
        c = lax.broadcasted_iota(I32, (nq, SUBLANES), 1)
        s2 = jnp.where(c <= r, s2, NEG)
        update(s2, vn)
        out = acc_ref[...] / l_ref[...]
        zo = jnp.zeros((N_HEADS, D_ATT), F32)
        rows = [jnp.sum(jnp.where(hmask, out[t * N_HEADS:(t + 1) * N_HEADS, :], zo),
                        axis=0, keepdims=True) for t in range(n_new)]
        o_ref[...] = jnp.concatenate(rows, axis=0).astype(o_ref.dtype)


def _sample_attention(qb, kn, vn, lfn, cache_k, cache_v, cache_logf, page_table):
    n, s_new, _ = qb.shape
    n_pages = page_table.shape[1]
    n_phys = cache_k.shape[0]
    ck = cache_k.reshape(n_phys, PAGE_SIZE, D_ATT)
    cv = cache_v.reshape(n_phys, PAGE_SIZE, D_ATT)
    clf = cache_logf.transpose(0, 2, 1)
    lfn_t = lfn.transpose(0, 2, 1)
    triu = jnp.triu(jnp.ones((PAGE_SIZE, PAGE_SIZE), BF16))
    nq = s_new * N_HEADS
    page = lambda i, p, pt: (pt[i * n_pages + p], 0, 0)
    seq = lambda i, p, pt: (i, 0, 0)
    gs = pltpu.PrefetchScalarGridSpec(
        num_scalar_prefetch=1,
        grid=(n, n_pages),
        in_specs=[pl.BlockSpec((None, s_new, D_ATT), seq),
                  pl.BlockSpec((None, PAGE_SIZE, D_ATT), page),
                  pl.BlockSpec((None, PAGE_SIZE, D_ATT), page),
                  pl.BlockSpec((None, N_HEADS, PAGE_SIZE), page),
                  pl.BlockSpec((None, s_new, D_ATT), seq),
                  pl.BlockSpec((None, s_new, D_ATT), seq),
                  pl.BlockSpec((None, N_HEADS, s_new), seq),
                  pl.BlockSpec((PAGE_SIZE, PAGE_SIZE), lambda i, p, pt: (0, 0))],
        out_specs=pl.BlockSpec((None, s_new, D_ATT), seq),
        scratch_shapes=[pltpu.VMEM((nq, 1), F32), pltpu.VMEM((nq, 1), F32),
                        pltpu.VMEM((nq, D_ATT), F32), pltpu.VMEM((N_HEADS, PAGE_SIZE), F32)],
    )
    return pl.pallas_call(
        functools.partial(_sattn_kernel, n_new=s_new),
        grid_spec=gs,
        out_shape=jax.ShapeDtypeStruct((n, s_new, D_ATT), F32),
        compiler_params=_cp(("parallel", "arbitrary")),
        name="sample_attn",
    )(page_table.reshape(-1), qb, ck, cv, clf, kn, vn, lfn_t, triu)


def _ln_swish(y, g, b):
    mu = jnp.mean(y, axis=-1, keepdims=True)
    yc = y - mu
    var = jnp.mean(yc * yc, axis=-1, keepdims=True)
    yn = yc * lax.rsqrt(var + EPS) * g + b
    return yn * jax.nn.sigmoid(yn)


def _conv_kernel(u_ref, cw_ref, cb_ref, g_ref, b_ref, o_ref, ext_ref, *, tm, hist):
    t = pl.program_id(1)

    @pl.when(t == 0)
    def _():
        ext_ref[0:hist, :] = jnp.zeros((hist, ext_ref.shape[1]), F32)

    ext_ref[hist:hist + tm, :] = u_ref[...]
    off = hist - (CONV_WIDTH - 1)
    y = jnp.zeros(u_ref.shape, F32) + cb_ref[...]
    for w in range(CONV_WIDTH):
        y = y + ext_ref[off + w:off + w + tm, :] * cw_ref[w:w + 1, :]
    o_ref[...] = _ln_swish(y, g_ref[...], b_ref[...]).astype(o_ref.dtype)
    ext_ref[0:hist, :] = ext_ref[tm:tm + hist, :]


def _conv_prompt(u, cw, cb, g, b, tm=512):
    bsz, t, c = u.shape
    hist = 32
    row = lambda bb, i: (bb, i, 0)
    const = lambda bb, i: (0, 0)
    return pl.pallas_call(
        functools.partial(_conv_kernel, tm=tm, hist=hist),
        grid=(bsz, t // tm),
        in_specs=[pl.BlockSpec((None, tm, c), row), pl.BlockSpec(cw.shape, const),
                  pl.BlockSpec((1, c), const), pl.BlockSpec((1, c), const),
                  pl.BlockSpec((1, c), const)],
        out_specs=pl.BlockSpec((None, tm, c), row),
        out_shape=jax.ShapeDtypeStruct((bsz, t, c), BF16),
        scratch_shapes=[pltpu.VMEM((hist + tm, c), F32)],
        compiler_params=_cp(("parallel", "arbitrary")),
        name="conv_prompt",
    )(u, cw, cb, g, b)


def _conv_step_kernel(st_ref, u_ref, cw_ref, cb_ref, g_ref, b_ref, o_ref, *, n_hist, n_new):
    def ext(i):
        return st_ref[i] if i < n_hist else u_ref[i - n_hist]

    for t in range(n_new):
        y = jnp.zeros(o_ref.shape[1:], F32) + cb_ref[...]
        for w in range(CONV_WIDTH):
            y = y + ext(t + w) * cw_ref[w:w + 1, :]
        o_ref[t] = _ln_swish(y, g_ref[...], b_ref[...]).astype(o_ref.dtype)


def _conv_sample(state_t, u_t, cw, cb, g, b):
    n_hist, n, c = state_t.shape
    n_new = u_t.shape[0]
    return pl.pallas_call(
        functools.partial(_conv_step_kernel, n_hist=n_hist, n_new=n_new),
        out_shape=jax.ShapeDtypeStruct((n_new, n, c), BF16),
        compiler_params=pltpu.CompilerParams(vmem_limit_bytes=VMEM_LIMIT),
        name="conv_sample",
    )(state_t, u_t, cw, cb, g, b)


def _tail_kernel(x_ref, a_ref, c_ref, g1_ref, sh_ref, sc_ref, n2_ref, woa_ref, woc_ref,
                 x1_ref, h2_ref):
    proj = (_dot(a_ref[...].astype(BF16), woa_ref[...])
            + _dot(c_ref[...].astype(BF16), woc_ref[...]))
    x1 = x_ref[...] + g1_ref[...] * proj
    x1_ref[...] = x1
    ms = jnp.mean(x1 * x1, axis=-1, keepdims=True)
    h2 = x1 * lax.rsqrt(ms + EPS) * n2_ref[...]
    h2_ref[...] = h2 * (1.0 + sc_ref[...]) + sh_ref[...]


def _tail(x, attn, conv, gate1, shift2, scale2, n2g, woa, woc, per_token, tm=256):
    bsz, t, d = x.shape
    row = lambda b, i: (b, i, 0)
    const2 = lambda b, i: (0, 0)
    mod_spec = (pl.BlockSpec((None, tm, d), row) if per_token
                else pl.BlockSpec((None, 1, d), lambda b, i: (b, 0, 0)))
    half = pl.BlockSpec((None, tm, D_ATT), row)
    full = pl.BlockSpec((None, tm, d), row)
    return pl.pallas_call(
        _tail_kernel,
        grid=(bsz, t // tm),
        in_specs=[full, half, half, mod_spec, mod_spec, mod_spec,
                  pl.BlockSpec(n2g.shape, const2), pl.BlockSpec(woa.shape, const2),
                  pl.BlockSpec(woc.shape, const2)],
        out_specs=[full, full],
        out_shape=[jax.ShapeDtypeStruct((bsz, t, d), F32)] * 2,
        compiler_params=_cp(("parallel", "parallel")),
        name="tail",
    )(x, attn, conv, gate1, shift2, scale2, n2g, woa, woc)


def _top_rows(s, payload, k):
    nrows = s.shape[0]
    iota = lax.broadcasted_iota(I32, s.shape, 0)
    vals, pays = [], []
    for _ in range(k):
        m = jnp.max(s, axis=0, keepdims=True)
        pos = jnp.min(jnp.where(s == m, iota, nrows), axis=0, keepdims=True)
        hit = iota == pos
        vals.append(m)
        pays.append(jnp.sum(jnp.where(hit, payload, 0), axis=0, keepdims=True))
        s = jnp.where(hit, -jnp.inf, s)
    return jnp.concatenate(vals, axis=0), jnp.concatenate(pays, axis=0)


def _route_kernel(h_ref, wq_ref, skh_ref, skl_ref, e_ref, g_ref):
    hb = h_ref[...].astype(BF16)
    tb = hb.shape[0]
    key_iota = lax.broadcasted_iota(I32, (PEER_N_KEYS, tb), 0)
    e_rows, g_rows = [], []
    for hd in range(PEER_HEADS):
        tops = []
        for half in range(2):
            i = hd * 2 + half
            wq = wq_ref[i * PEER_HALF:(i + 1) * PEER_HALF, :]
            qt = _dot_nt(wq, hb)
            qh, ql = _split2(qt)
            skh = skh_ref[i]
            s = _dot(skh, qh) + (_dot(skh, ql) + _dot(skl_ref[i], qh))
            tops.append(_top_rows(s, key_iota, PEER_TOPK))
        (s1, i1), (s2, i2) = tops
        cand = jnp.concatenate([s1[a:a + 1, :] + s2 for a in range(PEER_TOPK)], axis=0)
        cidx = jnp.concatenate([i1[a:a + 1, :] * PEER_N_KEYS + i2 for a in range(PEER_TOPK)],
                               axis=0)
        top, eidx = _top_rows(cand, cidx, PEER_TOPK)
        ex = jnp.exp(top - top[0:1, :])
        g_rows.append(ex / jnp.sum(ex, axis=0, keepdims=True))
        e_rows.append(eidx)
    e_ref[...] = jnp.concatenate(e_rows, axis=0).T
    g_ref[...] = jnp.concatenate(g_rows, axis=0).T


def _route(h2, wq_t, skh, skl, tb=128):
    n, d = h2.shape
    const2 = lambda i: (0, 0)
    const3 = lambda i: (0, 0, 0)
    return pl.pallas_call(
        _route_kernel,
        grid=(n // tb,),
        in_specs=[pl.BlockSpec((tb, d), lambda i: (i, 0)), pl.BlockSpec(wq_t.shape, const2),
                  pl.BlockSpec(skh.shape, const3), pl.BlockSpec(skl.shape, const3)],
        out_specs=[pl.BlockSpec((tb, PEER_SLOTS), lambda i: (i, 0))] * 2,
        out_shape=[jax.ShapeDtypeStruct((n, PEER_SLOTS), I32),
                   jax.ShapeDtypeStruct((n, PEER_SLOTS), F32)],
        compiler_params=_cp(("parallel",)),
        name="peer_route",
    )(h2, wq_t, skh, skl)


def _peer_act_kernel(idx_ref, tab_ref, h_ref, g_ref, ones_ref, c_ref, ps_ref, ss_ref, *, tb):
    def tok(t, carry):
        hv = h_ref[t]
        base = t * PEER_SLOTS
        for j in range(PEER_SLOTS):
            e = idx_ref[base + j]
            ps_ref[j * SUBLANES:(j + 1) * SUBLANES, :] = tab_ref[e].astype(F32) * hv
        s = ps_ref[pl.ds(0, PEER_SLOTS, stride=SUBLANES), :]
        for r in range(1, SUBLANES):
            s = s + ps_ref[pl.ds(r, PEER_SLOTS, stride=SUBLANES), :]
        ss_ref[pl.ds(pl.multiple_of(t * PEER_SLOTS, PEER_SLOTS), PEER_SLOTS), :] = s
        return carry

    lax.fori_loop(0, tb, tok, 0)
    sh, sl = _split2(ss_ref[...])
    ones = ones_ref[...]
    tot = _dot_nt(ones, sh) + _dot_nt(ones, sl)
    act = jnp.concatenate(
        [tot[0:1, t * PEER_SLOTS:(t + 1) * PEER_SLOTS] for t in range(tb)], axis=0)
    gelu = 0.5 * act * (1.0 + lax.erf(act * (2.0 ** -0.5)))
    c_ref[...] = g_ref[...] * gelu


def _peer_act(eidx, tab_u, h2, g, tb=32):
    n = eidx.shape[0]
    d = h2.shape[1]
    hc = h2.reshape(n, d // LANES, LANES)
    ones = jnp.ones((SUBLANES, LANES), BF16)
    return pl.pallas_call(
        functools.partial(_peer_act_kernel, tb=tb),
        grid=(n // tb,),
        in_specs=[pl.BlockSpec((tb * PEER_SLOTS,), lambda i: (i,), memory_space=pltpu.SMEM),
                  pl.BlockSpec(memory_space=pltpu.VMEM),
                  pl.BlockSpec((tb, d // LANES, LANES), lambda i: (i, 0, 0)),
                  pl.BlockSpec((tb, PEER_SLOTS), lambda i: (i, 0)),
                  pl.BlockSpec(ones.shape, lambda i: (0, 0))],
        out_specs=pl.BlockSpec((tb, PEER_SLOTS), lambda i: (i, 0)),
        out_shape=jax.ShapeDtypeStruct((n, PEER_SLOTS), F32),
        scratch_shapes=[pltpu.VMEM((PEER_SLOTS * SUBLANES, LANES), F32),
                        pltpu.VMEM((tb * PEER_SLOTS, LANES), F32)],
        compiler_params=_cp(("arbitrary",)),
        name="peer_act",
    )(eidx.reshape(-1), tab_u, hc, g, ones)


def _peer_out_kernel(idx_ref, c_ref, tab_ref, o_ref, *, tb):
    n_acc = 4

    def tok(t, carry):
        base = t * PEER_SLOTS
        accs = [jnp.zeros((SUBLANES, LANES), F32) for _ in range(n_acc)]
        for j in range(PEER_SLOTS):
            e = idx_ref[base + j]
            accs[j % n_acc] = accs[j % n_acc] + c_ref[base + j] * tab_ref[e].astype(F32)
        o_ref[t] = (accs[0] + accs[1]) + (accs[2] + accs[3])
        return carry

    lax.fori_loop(0, tb, tok, 0)


def _peer_out(eidx, coef, tab_v, tb=32):
    n = eidx.shape[0]
    chunks = tab_v.shape[1]
    smem = lambda: pl.BlockSpec((tb * PEER_SLOTS,), lambda i: (i,), memory_space=pltpu.SMEM)
    out = pl.pallas_call(
        functools.partial(_peer_out_kernel, tb=tb),
        grid=(n // tb,),
        in_specs=[smem(), smem(), pl.BlockSpec(memory_space=pltpu.VMEM)],
        out_specs=pl.BlockSpec((tb, chunks, LANES), lambda i: (i, 0, 0)),
        out_shape=jax.ShapeDtypeStruct((n, chunks, LANES), F32),
        compiler_params=_cp(("arbitrary",)),
        name="peer_out",
    )(eidx.reshape(-1), coef.reshape(-1), tab_v)
    return out.reshape(n, chunks * LANES)


def _final_kernel(x1_ref, g2_ref, p_ref, y_ref):
    y_ref[...] = x1_ref[...] + g2_ref[...] * p_ref[...]


def _final(x1, gate2, peer, per_token, tm=512):
    bsz, t, d = x1.shape
    row = lambda b, i: (b, i, 0)
    mod_spec = (pl.BlockSpec((None, tm, d), row) if per_token
                else pl.BlockSpec((None, 1, d), lambda b, i: (b, 0, 0)))
    full = pl.BlockSpec((None, tm, d), row)
    return pl.pallas_call(
        _final_kernel,
        grid=(bsz, t // tm),
        in_specs=[full, mod_spec, full],
        out_specs=full,
        out_shape=jax.ShapeDtypeStruct((bsz, t, d), F32),
        compiler_params=_cp(("parallel", "parallel")),
        name="final",
    )(x1, gate2, peer)


def _prep_inproj(w_in, b_f, b_glu, q_norm_g, k_norm_g, tm):
    d = D_ATT
    nf = 3 * d + N_HEADS
    w = jnp.concatenate([w_in[:, :3 * d], w_in[:, nf:]], axis=1).astype(BF16)
    wf = jnp.pad(w_in[:, 3 * d:nf], ((0, 0), (0, LANES - N_HEADS)))
    wfh = wf.astype(BF16)
    wfl = (wf - wfh.astype(F32)).astype(BF16)
    bfp = jnp.pad(b_f, (0, LANES - N_HEADS)).reshape(1, LANES)
    qg = jnp.tile(q_norm_g, N_HEADS).reshape(1, d)
    kg = jnp.tile(k_norm_g, N_HEADS).reshape(1, d)
    hid = jnp.arange(d) // HEAD_DIM
    ones_bd = (hid[:, None] == hid[None, :]).astype(BF16)
    tri = jnp.tril(jnp.ones((tm, tm), BF16))
    return (w, wfh, wfl, bfp, b_glu.reshape(1, -1), qg, kg, ones_bd, tri)


def _peer_ffn(h2, wq_t, skh, skl, tab_u, tab_v):
    eidx, g = _route(h2, wq_t, skh, skl)
    coef = _peer_act(eidx, tab_u, h2, g)
    return _peer_out(eidx, coef, tab_v)


def kernel(x_prompt, x_sample, c_prompt, c_sample, cache_k, cache_v, cache_logf, state_conv,
           page_table, norm1_g, norm2_g, w_ada, b_ada, w_in, b_f, b_glu, q_norm_g, k_norm_g,
           conv_w, conv_b, conv_ln_g, conv_ln_b, w_o, peer_w_query, peer_sub_keys, peer_u,
           peer_v):
    depth = w_ada.shape[0]
    bsz, seq, d = x_prompt.shape
    n_dec, s_new, _ = x_sample.shape
    tm = 256
    xp, xs = x_prompt, x_sample.reshape(1, n_dec * s_new, d)
    outs = [[] for _ in range(8)]
    for l in range(depth):
        n_c = bsz + n_dec
        pad = (-n_c) % SUBLANES
        c_all = jnp.pad(jnp.concatenate([c_prompt, c_sample], axis=0), ((0, pad), (0, 0)))
        mod = _adaln(c_all, w_ada[l], b_ada[l])
        mod_p = [m.reshape(bsz, 1, d) for m in jnp.split(mod[:bsz], 6, axis=-1)]
        mod_s = [jnp.repeat(m, s_new, axis=0).reshape(1, n_dec * s_new, d)
                 for m in jnp.split(mod[bsz:n_c], 6, axis=-1)]

        wts = _prep_inproj(w_in[l], b_f[l], b_glu[l], q_norm_g[l], k_norm_g[l], tm)
        g1 = norm1_g[l].reshape(1, d)
        n2 = norm2_g[l].reshape(1, d)
        cw = conv_w[l]
        cb, lg, lb = (a[l].reshape(1, -1) for a in (conv_b, conv_ln_g, conv_ln_b))
        woa = w_o[l][:D_ATT].astype(BF16)
        woc = w_o[l][D_ATT:].astype(BF16)
        wq_t = peer_w_query[l].T.astype(BF16)
        sk = peer_sub_keys[l].reshape(PEER_HEADS * 2, PEER_N_KEYS, PEER_HALF)
        skh = sk.astype(BF16)
        skl = (sk - skh.astype(F32)).astype(BF16)
        tab_u = peer_u[l].astype(BF16).reshape(-1, d // LANES, LANES)
        tab_v = peer_v[l].astype(BF16).reshape(-1, d // LANES, LANES)

        sh1, sc1, gt1, sh2, sc2, gt2 = mod_p
        qb, k, v, kb, vb, lf, fc, u = _inproj(xp, sh1, sc1, g1, wts, per_token=False, tm=tm)
        a_out = _prompt_attention(qb, kb, vb, fc)
        c_out = _conv_prompt(u, cw, cb, lg, lb)
        x1, h2 = _tail(xp, a_out, c_out, gt1, sh2, sc2, n2, woa, woc, per_token=False, tm=tm)
        peer = _peer_ffn(h2.reshape(bsz * seq, d), wq_t, skh, skl, tab_u, tab_v)
        xp = _final(x1, gt2, peer.reshape(bsz, seq, d), per_token=False)
        outs[0].append(k.reshape(bsz, seq, N_HEADS, HEAD_DIM))
        outs[1].append(v.reshape(bsz, seq, N_HEADS, HEAD_DIM))
        outs[2].append(lf)
        outs[3].append(u[:, seq - (CONV_WIDTH - 1):, :])

        sh1, sc1, gt1, sh2, sc2, gt2 = mod_s
        qb, k, v, kb, vb, lf, fc, u = _inproj(xs, sh1, sc1, g1, wts, per_token=True, tm=tm)
        shp = (n_dec, s_new, D_ATT)
        a_out = _sample_attention(qb.reshape(shp).astype(F32), k.reshape(shp), v.reshape(shp),
                                  lf.reshape(n_dec, s_new, N_HEADS), cache_k[l], cache_v[l],
                                  cache_logf[l], page_table)
        u_s = u.reshape(shp)
        c_out = _conv_sample(state_conv[l].transpose(1, 0, 2), u_s.transpose(1, 0, 2),
                             cw, cb, lg, lb).transpose(1, 0, 2)
        x1, h2 = _tail(xs, a_out.reshape(1, -1, D_ATT), c_out.reshape(1, -1, D_ATT), gt1, sh2,
                       sc2, n2, woa, woc, per_token=True, tm=tm)
        peer = _peer_ffn(h2.reshape(n_dec * s_new, d), wq_t, skh, skl, tab_u, tab_v)
        xs = _final(x1, gt2, peer.reshape(1, n_dec * s_new, d), per_token=True)
        outs[4].append(k.reshape(n_dec, s_new, N_HEADS, HEAD_DIM))
        outs[5].append(v.reshape(n_dec, s_new, N_HEADS, HEAD_DIM))
        outs[6].append(lf.reshape(n_dec, s_new, N_HEADS))
        outs[7].append(jnp.concatenate([state_conv[l], u_s], axis=1)[:, -(CONV_WIDTH - 1):])
    st = [jnp.stack(o) for o in outs]
    return (xp, xs.reshape(n_dec, s_new, d), st[0], st[1], st[2], st[3], st[4], st[5], st[6],
            st[7])
```

```python
import functools

import jax
import jax.numpy as jnp
from jax import lax
from jax.experimental import pallas as pl
from jax.experimental.pallas import tpu as pltpu

F32 = jnp.float32
BF16 = jnp.bfloat16
I32 = jnp.int32

EPS = 1e-6
N_HEADS = 8
HEAD_DIM = 64
D_ATT = N_HEADS * HEAD_DIM
CONV_WIDTH = 31
PAGE_SIZE = 128
PEER_HEADS = 8
PEER_N_KEYS = 128
PEER_HALF = 128
PEER_TOPK = 16
PEER_SLOTS = PEER_HEADS * PEER_TOPK
LANES = 128
SUBLANES = 8
NEG = -1e30
VMEM_LIMIT = 56 * 1024 * 1024


def _cp(sem, vmem=VMEM_LIMIT):
    return pltpu.CompilerParams(dimension_semantics=sem, vmem_limit_bytes=vmem)


def _dot(a, b):
    return jnp.dot(a, b, preferred_element_type=F32)


def _dot_nt(a, b):
    return lax.dot_general(a, b, (((1,), (1,)), ((), ())), preferred_element_type=F32)


def _split2(a):
    hi = a.astype(BF16)
    lo = (a - hi.astype(F32)).astype(BF16)
    return hi, lo


def _split3(a):
    hi = a.astype(BF16)
    r = a - hi.astype(F32)
    mid = r.astype(BF16)
    lo = (r - mid.astype(F32)).astype(BF16)
    return hi, mid, lo


def _log_sigmoid(x):
    return jnp.minimum(x, 0.0) - jnp.log1p(jnp.exp(-jnp.abs(x)))


def _ada_kernel(c_ref, w_ref, b_ref, o_ref):
    ch, cl = _split2(c_ref[...])
    wh, wl = _split2(w_ref[...])
    o_ref[...] = _dot(ch, wh) + (_dot(ch, wl) + _dot(cl, wh)) + b_ref[...]


def _adaln(c, w_ada, b_ada):
    r, d = c.shape
    n = w_ada.shape[1]
    tn = 512
    return pl.pallas_call(
        _ada_kernel,
        grid=(n // tn,),
        in_specs=[pl.BlockSpec((r, d), lambda j: (0, 0)),
                  pl.BlockSpec((d, tn), lambda j: (0, j)),
                  pl.BlockSpec((1, tn), lambda j: (0, j))],
        out_specs=pl.BlockSpec((r, tn), lambda j: (0, j)),
        out_shape=jax.ShapeDtypeStruct((r, n), F32),
        compiler_params=_cp(("parallel",)),
        name="adaln",
    )(c, w_ada, b_ada.reshape(1, n))


def _inproj_kernel(x_ref, sh_ref, sc_ref, g1_ref, w_ref, wfh_ref, wfl_ref, bf_ref, bglu_ref,
                   qg_ref, kg_ref, ones_ref, tri_ref,
                   q_ref, k_ref, v_ref, kb_ref, vb_ref, lf_ref, fc_ref, u_ref, carry_ref):
    t = pl.program_id(1)

    @pl.when(t == 0)
    def _():
        carry_ref[...] = jnp.zeros_like(carry_ref)

    x = x_ref[...]
    ms = jnp.mean(x * x, axis=-1, keepdims=True)
    h = x * lax.rsqrt(ms + EPS) * g1_ref[...]
    h = h * (1.0 + sc_ref[...]) + sh_ref[...]
    hh, hl = _split2(h)
    z = _dot(hh, w_ref[...])
    fg = _dot(hh, wfh_ref[...]) + (_dot(hh, wfl_ref[...]) + _dot(hl, wfh_ref[...]))
    d = D_ATT
    q = z[:, 0:d]
    k = z[:, d:2 * d]
    v = z[:, 2 * d:3 * d]
    a = z[:, 3 * d:4 * d] + bglu_ref[:, 0:d]
    b = z[:, 4 * d:5 * d] + bglu_ref[:, d:2 * d]
    qs = _dot((q * q).astype(BF16), ones_ref[...])
    ks = _dot((k * k).astype(BF16), ones_ref[...])
    qn = q * lax.rsqrt(qs * (1.0 / HEAD_DIM) + EPS) * qg_ref[...]
    kn = k * lax.rsqrt(ks * (1.0 / HEAD_DIM) + EPS) * kg_ref[...]
    q_ref[...] = (qn * (HEAD_DIM ** -0.5)).astype(BF16)
    k_ref[...] = kn
    kb_ref[...] = kn.astype(BF16)
    v_ref[...] = v
    vb_ref[...] = v.astype(BF16)
    u_ref[...] = a * jax.nn.sigmoid(b)
    lf = _log_sigmoid(fg + bf_ref[...])
    lf_ref[...] = lf[:, 0:N_HEADS]
    l1, l2, l3 = _split3(lf)
    tri = tri_ref[...]
    fc = _dot(tri, l1) + (_dot(tri, l2) + _dot(tri, l3)) + carry_ref[...]
    fc_ref[...] = fc[:, 0:N_HEADS]
    tm = fc.shape[0]
    carry_ref[...] = fc[tm - 1:tm, :]


def _inproj(x, shift, scale, g1, wts, per_token, tm=256):
    bsz, t, d = x.shape
    w, wfh, wfl, bfp, bglu, qg, kg, ones_bd, tri = wts
    row = lambda b, i: (b, i, 0)
    const2 = lambda b, i: (0, 0)
    mod_spec = (pl.BlockSpec((None, tm, d), row) if per_token
                else pl.BlockSpec((None, 1, d), lambda b, i: (b, 0, 0)))
    full = lambda arr: pl.BlockSpec(arr.shape, const2)
    outs = [
        jax.ShapeDtypeStruct((bsz, t, D_ATT), BF16),
        jax.ShapeDtypeStruct((bsz, t, D_ATT), F32),
        jax.ShapeDtypeStruct((bsz, t, D_ATT), F32),
        jax.ShapeDtypeStruct((bsz, t, D_ATT), BF16),
        jax.ShapeDtypeStruct((bsz, t, D_ATT), BF16),
        jax.ShapeDtypeStruct((bsz, t, N_HEADS), F32),
        jax.ShapeDtypeStruct((bsz, t, N_HEADS), F32),
        jax.ShapeDtypeStruct((bsz, t, D_ATT), F32),
    ]
    ospec = lambda s: pl.BlockSpec((None, tm, s.shape[2]), row)
    return pl.pallas_call(
        _inproj_kernel,
        grid=(bsz, t // tm),
        in_specs=[pl.BlockSpec((None, tm, d), row), mod_spec, mod_spec, full(g1), full(w),
                  full(wfh), full(wfl), full(bfp), full(bglu), full(qg), full(kg),
                  full(ones_bd), full(tri)],
        out_specs=[ospec(s) for s in outs],
        out_shape=outs,
        scratch_shapes=[pltpu.VMEM((1, LANES), F32)],
        compiler_params=_cp(("parallel", "arbitrary")),
        name="inproj",
    )(x, shift, scale, g1, w, wfh, wfl, bfp, bglu, qg, kg, ones_bd, tri)


def _attn_kernel(q_ref, k_ref, v_ref, f_ref, o_ref, acc_ref, *, tq, tk):
    qi = pl.program_id(2)
    q = q_ref[...]
    lane = lax.broadcasted_iota(I32, (1, LANES), 1)
    zero = jnp.zeros_like(q)
    qm = [jnp.where(lane < HEAD_DIM, q, zero), jnp.where(lane >= HEAD_DIM, q, zero)]
    acc_ref[...] = jnp.zeros_like(acc_ref)
    rows = lax.broadcasted_iota(I32, (tq, tk), 0)
    cols = lax.broadcasted_iota(I32, (tq, tk), 1)

    def step(j, carry, diag_off=None):
        start = pl.multiple_of(j * tk, tk)
        kc = k_ref[pl.ds(start, tk), :]
        vc = v_ref[pl.ds(start, tk), :]
        fr = f_ref[j]
        new = []
        for hh in range(2):
            m, l = carry[2 * hh], carry[2 * hh + 1]
            s = _dot_nt(qm[hh], kc) - fr[hh:hh + 1, :]
            if diag_off is not None:
                s = jnp.where(cols + diag_off <= rows, s, NEG)
            m_new = jnp.maximum(m, jnp.max(s, axis=1, keepdims=True))
            alpha = jnp.exp(m - m_new)
            p = jnp.exp(s - m_new)
            l_new = alpha * l + jnp.sum(p, axis=1, keepdims=True)
            acc_ref[hh] = alpha * acc_ref[hh] + _dot(p.astype(BF16), vc)
            new += [m_new, l_new]
        return tuple(new)

    init = tuple(jnp.full((tq, 1), NEG, F32) if i % 2 == 0 else jnp.zeros((tq, 1), F32)
                 for i in range(4))
    n_diag = tq // tk
    carry = lax.fori_loop(0, qi * n_diag, step, init)
    for dd in range(n_diag):
        carry = step(qi * n_diag + dd, carry, diag_off=dd * tk)
    o0 = acc_ref[0] / carry[1]
    o1 = acc_ref[1] / carry[3]
    o_ref[...] = jnp.where(lane < HEAD_DIM, o0, o1).astype(o_ref.dtype)


def _prompt_attention(qb, kb, vb, fcum, tq=512, tk=256):
    bsz, t, _ = qb.shape
    n_pairs = N_HEADS // 2
    nk = t // tk
    f_rows = fcum.transpose(0, 2, 1).reshape(bsz, n_pairs, 2, nk, tk).transpose(0, 1, 3, 2, 4)
    return pl.pallas_call(
        functools.partial(_attn_kernel, tq=tq, tk=tk),
        grid=(bsz, n_pairs, t // tq),
        in_specs=[pl.BlockSpec((None, tq, LANES), lambda b, p, i: (b, i, p)),
                  pl.BlockSpec((None, t, LANES), lambda b, p, i: (b, 0, p)),
                  pl.BlockSpec((None, t, LANES), lambda b, p, i: (b, 0, p)),
                  pl.BlockSpec((None, None, nk, 2, tk), lambda b, p, i: (b, p, 0, 0, 0))],
        out_specs=pl.BlockSpec((None, tq, LANES), lambda b, p, i: (b, i, p)),
        out_shape=jax.ShapeDtypeStruct((bsz, t, D_ATT), BF16),
        scratch_shapes=[pltpu.VMEM((2, tq, LANES), F32)],
        compiler_params=_cp(("parallel", "parallel", "arbitrary")),
        name="prompt_attn",
    )(qb, kb, vb, f_rows)


def _sattn_kernel(pt_ref, q_ref, *refs, n_pg):
    ck = refs[0:n_pg]
    cv = refs[n_pg:2 * n_pg]
    clf = refs[2 * n_pg:3 * n_pg]
    (kn_ref, vn_ref, lfn_ref, triu_ref, pall_ref, tnew_ref, o_ref,
     m_ref, l_ref, acc_ref, fcar_ref) = refs[3 * n_pg:]
    p = pl.program_id(1)
    nq = q_ref.shape[0]
    flat = PAGE_SIZE * N_HEADS

    @pl.when(p == 0)
    def _():
        m_ref[...] = jnp.full_like(m_ref, NEG)
        l_ref[...] = jnp.zeros_like(l_ref)
        acc_ref[...] = jnp.zeros_like(acc_ref)
        fcar_ref[...] = jnp.zeros_like(fcar_ref)

    a = q_ref[...].astype(BF16)
    same_head = (lax.broadcasted_iota(I32, (nq, flat), 0) % N_HEADS
                 == lax.broadcasted_iota(I32, (nq, flat), 1) % N_HEADS)
    diag = (lax.broadcasted_iota(I32, (N_HEADS, flat), 0)
            == lax.broadcasted_iota(I32, (N_HEADS, flat), 1) % N_HEADS)
    pall = pall_ref[...]

    def flatten(f):
        f1, f2, f3 = _split3(f)
        e = _dot(f1, pall) + (_dot(f2, pall) + _dot(f3, pall))
        rows = []
        for g in range(f.shape[0] // N_HEADS):
            eg = e[g * N_HEADS:(g + 1) * N_HEADS, :]
            rows.append(jnp.sum(jnp.where(diag, eg, 0.0), axis=0, keepdims=True))
        return rows

    def update(scores, values):
        m = m_ref[...]
        m_new = m
        for s in scores:
            m_new = jnp.maximum(m_new, jnp.max(s, axis=1, keepdims=True))
        alpha = jnp.exp(m - m_new)
        l_new = alpha * l_ref[...]
        acc = alpha * acc_ref[...]
        for s, vmat in zip(scores, values):
            pr = jnp.exp(s - m_new)
            l_new = l_new + jnp.sum(pr, axis=1, keepdims=True)
            acc = acc + _dot(pr.astype(BF16), vmat)
        l_ref[...] = l_new
        acc_ref[...] = acc
        m_ref[...] = m_new

    l1, l2, l3 = _split3(jnp.concatenate([r[...] for r in clf], axis=0))
    tri = triu_ref[...]
    local = _dot(l1, tri) + (_dot(l2, tri) + _dot(l3, tri))
    fcar = fcar_ref[...]
    fpages = []
    for g in range(n_pg):
        fp = local[g * N_HEADS:(g + 1) * N_HEADS, :] + fcar
        fpages.append(fp)
        fcar = jnp.broadcast_to(fp[:, PAGE_SIZE - 1:PAGE_SIZE], fcar.shape)
    fcar_ref[...] = fcar
    frows = flatten(jnp.concatenate(fpages, axis=0))
    scores = [jnp.where(same_head, _dot_nt(a, ck[g][...].astype(BF16)) - frows[g], NEG)
              for g in range(n_pg)]
    update(scores, [cv[g][...].astype(BF16) for g in range(n_pg)])

    @pl.when(p == pl.num_programs(1) - 1)
    def _():
        t1, t2, t3 = _split3(lfn_ref[...])
        tnew = tnew_ref[...]
        fnew = (flatten(fcar)[0][:, 0:nq]
                + (_dot(t1, tnew) + (_dot(t2, tnew) + _dot(t3, tnew))))
        r = lax.broadcasted_iota(I32, (nq, nq), 0)
        c = lax.broadcasted_iota(I32, (nq, nq), 1)
        visible = (r % N_HEADS == c % N_HEADS) & (c // N_HEADS <= r // N_HEADS)
        s2 = _dot_nt(a, kn_ref[...].astype(BF16)) - fnew
        update([jnp.where(visible, s2, NEG)], [vn_ref[...].astype(BF16)])
        o_ref[...] = acc_ref[...] / l_ref[...]


def _sample_attention(qb, kn, vn, lfn, cache_k, cache_v, cache_logf, page_table, n_pg=4):
    n, s_new, _ = qb.shape
    n_pages = page_table.shape[1]
    n_phys = cache_k.shape[0]
    nq = s_new * N_HEADS
    flat = PAGE_SIZE * N_HEADS
    ck = cache_k.reshape(n_phys, flat, HEAD_DIM)
    cv = cache_v.reshape(n_phys, flat, HEAD_DIM)
    clf = cache_logf.transpose(0, 2, 1)
    triu = jnp.triu(jnp.ones((PAGE_SIZE, PAGE_SIZE), BF16))
    col = jnp.arange(flat)
    pall = (jnp.arange(PAGE_SIZE)[:, None] == col[None, :] // N_HEADS).astype(BF16)
    tok = jnp.arange(nq)
    tnew = ((tok[:, None] % N_HEADS == tok[None, :] % N_HEADS)
            & (tok[:, None] <= tok[None, :])).astype(BF16)
    rows = lambda x: x.reshape(n, nq, HEAD_DIM)
    seq = lambda i, p, pt: (i, 0, 0)
    const = lambda i, p, pt: (0, 0)
    row_spec = pl.BlockSpec((None, nq, HEAD_DIM), seq)

    def page(g):
        return lambda i, p, pt: (pt[i * n_pages + p * n_pg + g], 0, 0)

    kv_specs = [pl.BlockSpec((None, flat, HEAD_DIM), page(g)) for g in range(n_pg)]
    lf_specs = [pl.BlockSpec((None, N_HEADS, PAGE_SIZE), page(g)) for g in range(n_pg)]
    gs = pltpu.PrefetchScalarGridSpec(
        num_scalar_prefetch=1,
        grid=(n, n_pages // n_pg),
        in_specs=[row_spec] + kv_specs + kv_specs + lf_specs + [
            row_spec, row_spec,
            pl.BlockSpec((None, 1, nq), seq),
            pl.BlockSpec(triu.shape, const), pl.BlockSpec(pall.shape, const),
            pl.BlockSpec(tnew.shape, const)],
        out_specs=row_spec,
        scratch_shapes=[pltpu.VMEM((nq, 1), F32), pltpu.VMEM((nq, 1), F32),
                        pltpu.VMEM((nq, HEAD_DIM), F32),
                        pltpu.VMEM((N_HEADS, PAGE_SIZE), F32)],
    )
    out = pl.pallas_call(
        functools.partial(_sattn_kernel, n_pg=n_pg),
        grid_spec=gs,
        out_shape=jax.ShapeDtypeStruct((n, nq, HEAD_DIM), F32),
        compiler_params=_cp(("parallel", "arbitrary")),
        name="sample_attn",
    )(page_table.reshape(-1), rows(qb), *([ck] * n_pg), *([cv] * n_pg), *([clf] * n_pg),
      rows(kn), rows(vn), lfn.reshape(n, 1, nq), triu, pall, tnew)
    return out.reshape(n, s_new, D_ATT)


def _ln_swish(y, g, b):
    mu = jnp.mean(y, axis=-1, keepdims=True)
    yc = y - mu
    var = jnp.mean(yc * yc, axis=-1, keepdims=True)
    yn = yc * lax.rsqrt(var + EPS) * g + b
    return yn * jax.nn.sigmoid(yn)


def _conv_kernel(u_ref, cw_ref, cb_ref, g_ref, b_ref, o_ref, ext_ref, *, tm, hist):
    t = pl.program_id(1)

    @pl.when(t == 0)
    def _():
        ext_ref[0:hist, :] = jnp.zeros((hist, ext_ref.shape[1]), F32)

    ext_ref[hist:hist + tm, :] = u_ref[...]
    off = hist - (CONV_WIDTH - 1)
    y = jnp.zeros(u_ref.shape, F32) + cb_ref[...]
    for w in range(CONV_WIDTH):
        y = y + ext_ref[off + w:off + w + tm, :] * cw_ref[w:w + 1, :]
    o_ref[...] = _ln_swish(y, g_ref[...], b_ref[...]).astype(o_ref.dtype)
    ext_ref[0:hist, :] = ext_ref[tm:tm + hist, :]


def _conv_prompt(u, cw, cb, g, b, tm=512):
    bsz, t, c = u.shape
    hist = 32
    row = lambda bb, i: (bb, i, 0)
    const = lambda bb, i: (0, 0)
    return pl.pallas_call(
        functools.partial(_conv_kernel, tm=tm, hist=hist),
        grid=(bsz, t // tm),
        in_specs=[pl.BlockSpec((None, tm, c), row), pl.BlockSpec(cw.shape, const),
                  pl.BlockSpec((1, c), const), pl.BlockSpec((1, c), const),
                  pl.BlockSpec((1, c), const)],
        out_specs=pl.BlockSpec((None, tm, c), row),
        out_shape=jax.ShapeDtypeStruct((bsz, t, c), BF16),
        scratch_shapes=[pltpu.VMEM((hist + tm, c), F32)],
        compiler_params=_cp(("parallel", "arbitrary")),
        name="conv_prompt",
    )(u, cw, cb, g, b)


def _conv_step_kernel(st_ref, u_ref, cw_ref, cb_ref, g_ref, b_ref, o_ref, *, n_hist, n_new):
    def ext(i):
        return st_ref[i] if i < n_hist else u_ref[i - n_hist]

    for t in range(n_new):
        y = jnp.zeros(o_ref.shape[1:], F32) + cb_ref[...]
        for w in range(CONV_WIDTH):
            y = y + ext(t + w) * cw_ref[w:w + 1, :]
        o_ref[t] = _ln_swish(y, g_ref[...], b_ref[...]).astype(o_ref.dtype)


def _conv_sample(state_t, u_t, cw, cb, g, b):
    n_hist, n, c = state_t.shape
    n_new = u_t.shape[0]
    return pl.pallas_call(
        functools.partial(_conv_step_kernel, n_hist=n_hist, n_new=n_new),
        out_shape=jax.ShapeDtypeStruct((n_new, n, c), BF16),
        compiler_params=pltpu.CompilerParams(vmem_limit_bytes=VMEM_LIMIT),
        name="conv_sample",
    )(state_t, u_t, cw, cb, g, b)


def _tail_kernel(x_ref, a_ref, c_ref, g1_ref, sh_ref, sc_ref, n2_ref, woa_ref, woc_ref,
                 x1_ref, h2_ref):
    proj = (_dot(a_ref[...].astype(BF16), woa_ref[...])
            + _dot(c_ref[...].astype(BF16), woc_ref[...]))
    x1 = x_ref[...] + g1_ref[...] * proj
    x1_ref[...] = x1
    ms = jnp.mean(x1 * x1, axis=-1, keepdims=True)
    h2 = x1 * lax.rsqrt(ms + EPS) * n2_ref[...]
    h2_ref[...] = h2 * (1.0 + sc_ref[...]) + sh_ref[...]


def _tail(x, attn, conv, gate1, shift2, scale2, n2g, woa, woc, per_token, tm=256):
    bsz, t, d = x.shape
    row = lambda b, i: (b, i, 0)
    const2 = lambda b, i: (0, 0)
    mod_spec = (pl.BlockSpec((None, tm, d), row) if per_token
                else pl.BlockSpec((None, 1, d), lambda b, i: (b, 0, 0)))
    half = pl.BlockSpec((None, tm, D_ATT), row)
    full = pl.BlockSpec((None, tm, d), row)
    return pl.pallas_call(
        _tail_kernel,
        grid=(bsz, t // tm),
        in_specs=[full, half, half, mod_spec, mod_spec, mod_spec,
                  pl.BlockSpec(n2g.shape, const2), pl.BlockSpec(woa.shape, const2),
                  pl.BlockSpec(woc.shape, const2)],
        out_specs=[full, full],
        out_shape=[jax.ShapeDtypeStruct((bsz, t, d), F32)] * 2,
        compiler_params=_cp(("parallel", "parallel")),
        name="tail",
    )(x, attn, conv, gate1, shift2, scale2, n2g, woa, woc)


def _top_rows(s, order, payload, k):
    big = float(2 ** 24)
    vals, outs = [], []
    for _ in range(k):
        m = jnp.max(s, axis=0, keepdims=True)
        pos = jnp.min(jnp.where(s == m, order, big), axis=0, keepdims=True)
        hit = order == pos
        vals.append(m)
        if payload is None:
            outs.append(pos)
        else:
            outs.append(jnp.sum(jnp.where(hit, payload, 0), axis=0, keepdims=True))
        s = jnp.where(hit, -jnp.inf, s)
    return jnp.concatenate(vals, axis=0), jnp.concatenate(outs, axis=0)


def _pair_groups():
    groups = [(0, PEER_TOPK, PEER_TOPK)]
    for a in range(1, SUBLANES):
        groups.append((a, SUBLANES, PEER_TOPK // (a + 1)))
    groups.append((None, SUBLANES, SUBLANES))
    return groups


def _route_kernel(h_ref, wq_ref, skh_ref, skl_ref, e_ref, g_ref):
    hb = h_ref[...].astype(BF16)
    tb = hb.shape[0]
    key_order = lax.broadcasted_iota(I32, (PEER_N_KEYS, tb), 0).astype(F32)
    groups = _pair_groups()
    orders, masks = [], []
    for a, rows, valid in groups:
        r = lax.broadcasted_iota(I32, (rows, tb), 0)
        flat = (a * PEER_TOPK + r) if a is not None else (SUBLANES + r) * PEER_TOPK
        orders.append(flat.astype(F32))
        masks.append(r < valid)
    pair_order = jnp.concatenate(orders, axis=0)
    pair_valid = jnp.concatenate(masks, axis=0)
    e_rows, g_rows = [], []
    for hd in range(PEER_HEADS):
        tops = []
        for half in range(2):
            i = hd * 2 + half
            wq = wq_ref[i * PEER_HALF:(i + 1) * PEER_HALF, :]
            qt = _dot_nt(wq, hb)
            qh, ql = _split2(qt)
            skh = skh_ref[i]
            s = _dot(skh, qh) + (_dot(skh, ql) + _dot(skl_ref[i], qh))
            vals, pos = _top_rows(s, key_order, None, PEER_TOPK)
            tops.append((vals, pos.astype(I32)))
        (s1, i1), (s2, i2) = tops
        cand, cidx = [], []
        for a, rows, _ in groups:
            if a is not None:
                cand.append(s1[a:a + 1, :] + s2[0:rows, :])
                cidx.append(i1[a:a + 1, :] * PEER_N_KEYS + i2[0:rows, :])
            else:
                cand.append(s1[SUBLANES:, :] + s2[0:1, :])
                cidx.append(i1[SUBLANES:, :] * PEER_N_KEYS + i2[0:1, :])
        cand = jnp.where(pair_valid, jnp.concatenate(cand, axis=0), -jnp.inf)
        top, eidx = _top_rows(cand, pair_order, jnp.concatenate(cidx, axis=0), PEER_TOPK)
        ex = jnp.exp(top - top[0:1, :])
        g_rows.append(ex / jnp.sum(ex, axis=0, keepdims=True))
        e_rows.append(eidx)
    e_ref[...] = jnp.concatenate(e_rows, axis=0).T
    g_ref[...] = jnp.concatenate(g_rows, axis=0).T


def _route(h2, wq_t, skh, skl, tb=128):
    n, d = h2.shape
    const2 = lambda i: (0, 0)
    const3 = lambda i: (0, 0, 0)
    return pl.pallas_call(
        _route_kernel,
        grid=(n // tb,),
        in_specs=[pl.BlockSpec((tb, d), lambda i: (i, 0)), pl.BlockSpec(wq_t.shape, const2),
                  pl.BlockSpec(skh.shape, const3), pl.BlockSpec(skl.shape, const3)],
        out_specs=[pl.BlockSpec((tb, PEER_SLOTS), lambda i: (i, 0))] * 2,
        out_shape=[jax.ShapeDtypeStruct((n, PEER_SLOTS), I32),
                   jax.ShapeDtypeStruct((n, PEER_SLOTS), F32)],
        compiler_params=_cp(("parallel",)),
        name="peer_route",
    )(h2, wq_t, skh, skl)


def _peer_act_kernel(idx_ref, tab_ref, h_ref, g_ref, c_ref, ps0_ref, ps1_ref, act_ref, *, tb):
    def gather(t, ps_ref):
        hv = h_ref[t]
        idx_t = idx_ref.at[pl.ds(t * PEER_SLOTS, PEER_SLOTS)]
        for j in range(PEER_SLOTS):
            ps_ref[j * SUBLANES:(j + 1) * SUBLANES, :] = tab_ref[idx_t[j]].astype(F32) * hv

    def reduce(t, ps_ref):
        s = ps_ref[pl.ds(0, PEER_SLOTS, stride=SUBLANES), :]
        for r in range(1, SUBLANES):
            s = s + ps_ref[pl.ds(r, PEER_SLOTS, stride=SUBLANES), :]
        act_ref[pl.ds(t, 1), :] = jnp.sum(s.T, axis=0, keepdims=True)

    def pair(i, carry):
        gather(2 * i, ps0_ref)
        reduce(jnp.maximum(2 * i - 1, 0), ps1_ref)
        gather(2 * i + 1, ps1_ref)
        reduce(2 * i, ps0_ref)
        return carry

    ps1_ref[...] = jnp.zeros_like(ps1_ref)
    lax.fori_loop(0, tb // 2, pair, 0)
    reduce(tb - 1, ps1_ref)
    act = act_ref[...]
    gelu = 0.5 * act * (1.0 + lax.erf(act * (2.0 ** -0.5)))
    c_ref[...] = g_ref[...] * gelu


def _peer_act(eidx, tab_u, h2, g, tb=64):
    n = eidx.shape[0]
    d = h2.shape[1]
    hc = h2.reshape(n, d // LANES, LANES)
    return pl.pallas_call(
        functools.partial(_peer_act_kernel, tb=tb),
        grid=(n // tb,),
        in_specs=[pl.BlockSpec((tb * PEER_SLOTS,), lambda i: (i,), memory_space=pltpu.SMEM),
                  pl.BlockSpec(memory_space=pltpu.VMEM),
                  pl.BlockSpec((tb, d // LANES, LANES), lambda i: (i, 0, 0)),
                  pl.BlockSpec((tb, PEER_SLOTS), lambda i: (i, 0))],
        out_specs=pl.BlockSpec((tb, PEER_SLOTS), lambda i: (i, 0)),
        out_shape=jax.ShapeDtypeStruct((n, PEER_SLOTS), F32),
        scratch_shapes=[pltpu.VMEM((PEER_SLOTS * SUBLANES, LANES), F32),
                        pltpu.VMEM((PEER_SLOTS * SUBLANES, LANES), F32),
                        pltpu.VMEM((tb, PEER_SLOTS), F32)],
        compiler_params=_cp(("arbitrary",)),
        name="peer_act",
    )(eidx.reshape(-1), tab_u, hc, g)


def _peer_out_kernel(idx_ref, tab_ref, c_ref, o_ref, cb_ref, *, tb):
    n_acc = 4

    def tok(t, carry):
        idx_t = idx_ref.at[pl.ds(t * PEER_SLOTS, PEER_SLOTS)]
        crow = c_ref[pl.ds(t, 1), :]
        cb_ref[...] = jnp.broadcast_to(crow, (PEER_SLOTS, PEER_SLOTS)).T
        accs = [jnp.zeros((SUBLANES, LANES), F32) for _ in range(n_acc)]
        for j in range(PEER_SLOTS):
            accs[j % n_acc] = (accs[j % n_acc]
                               + cb_ref[j:j + 1, :] * tab_ref[idx_t[j]].astype(F32))
        o_ref[t] = (accs[0] + accs[1]) + (accs[2] + accs[3])
        return carry

    lax.fori_loop(0, tb, tok, 0)


def _peer_out(eidx, coef, tab_v, tb=64):
    n = eidx.shape[0]
    chunks = tab_v.shape[1]
    out = pl.pallas_call(
        functools.partial(_peer_out_kernel, tb=tb),
        grid=(n // tb,),
        in_specs=[pl.BlockSpec((tb * PEER_SLOTS,), lambda i: (i,), memory_space=pltpu.SMEM),
                  pl.BlockSpec(memory_space=pltpu.VMEM),
                  pl.BlockSpec((tb, PEER_SLOTS), lambda i: (i, 0))],
        out_specs=pl.BlockSpec((tb, chunks, LANES), lambda i: (i, 0, 0)),
        out_shape=jax.ShapeDtypeStruct((n, chunks, LANES), F32),
        scratch_shapes=[pltpu.VMEM((PEER_SLOTS, PEER_SLOTS), F32)],
        compiler_params=_cp(("arbitrary",)),
        name="peer_out",
    )(eidx.reshape(-1), tab_v, coef)
    return out.reshape(n, chunks * LANES)


def _final_kernel(x1_ref, g2_ref, p_ref, y_ref):
    y_ref[...] = x1_ref[...] + g2_ref[...] * p_ref[...]


def _final(x1, gate2, peer, per_token, tm=512):
    bsz, t, d = x1.shape
    row = lambda b, i: (b, i, 0)
    mod_spec = (pl.BlockSpec((None, tm, d), row) if per_token
                else pl.BlockSpec((None, 1, d), lambda b, i: (b, 0, 0)))
    full = pl.BlockSpec((None, tm, d), row)
    return pl.pallas_call(
        _final_kernel,
        grid=(bsz, t // tm),
        in_specs=[full, mod_spec, full],
        out_specs=full,
        out_shape=jax.ShapeDtypeStruct((bsz, t, d), F32),
        compiler_params=_cp(("parallel", "parallel")),
        name="final",
    )(x1, gate2, peer)


def _prep_inproj(w_in, b_f, b_glu, q_norm_g, k_norm_g, tm):
    d = D_ATT
    nf = 3 * d + N_HEADS
    w = jnp.concatenate([w_in[:, :3 * d], w_in[:, nf:]], axis=1).astype(BF16)
    wf = jnp.pad(w_in[:, 3 * d:nf], ((0, 0), (0, LANES - N_HEADS)))
    wfh = wf.astype(BF16)
    wfl = (wf - wfh.astype(F32)).astype(BF16)
    bfp = jnp.pad(b_f, (0, LANES - N_HEADS)).reshape(1, LANES)
    qg = jnp.tile(q_norm_g, N_HEADS).reshape(1, d)
    kg = jnp.tile(k_norm_g, N_HEADS).reshape(1, d)
    hid = jnp.arange(d) // HEAD_DIM
    ones_bd = (hid[:, None] == hid[None, :]).astype(BF16)
    tri = jnp.tril(jnp.ones((tm, tm), BF16))
    return (w, wfh, wfl, bfp, b_glu.reshape(1, -1), qg, kg, ones_bd, tri)


def _peer_ffn(h2, wq_t, skh, skl, tab_u, tab_v):
    eidx, g = _route(h2, wq_t, skh, skl)
    coef = _peer_act(eidx, tab_u, h2, g)
    return _peer_out(eidx, coef, tab_v)


def kernel(x_prompt, x_sample, c_prompt, c_sample, cache_k, cache_v, cache_logf, state_conv,
           page_table, norm1_g, norm2_g, w_ada, b_ada, w_in, b_f, b_glu, q_norm_g, k_norm_g,
           conv_w, conv_b, conv_ln_g, conv_ln_b, w_o, peer_w_query, peer_sub_keys, peer_u,
           peer_v):
    depth = w_ada.shape[0]
    bsz, seq, d = x_prompt.shape
    n_dec, s_new, _ = x_sample.shape
    tm = 256
    xp, xs = x_prompt, x_sample.reshape(1, n_dec * s_new, d)
    outs = [[] for _ in range(8)]
    for l in range(depth):
        n_c = bsz + n_dec
        pad = (-n_c) % SUBLANES
        c_all = jnp.pad(jnp.concatenate([c_prompt, c_sample], axis=0), ((0, pad), (0, 0)))
        mod = _adaln(c_all, w_ada[l], b_ada[l])
        mod_p = [m.reshape(bsz, 1, d) for m in jnp.split(mod[:bsz], 6, axis=-1)]
        mod_s = [jnp.repeat(m, s_new, axis=0).reshape(1, n_dec * s_new, d)
                 for m in jnp.split(mod[bsz:n_c], 6, axis=-1)]

        wts = _prep_inproj(w_in[l], b_f[l], b_glu[l], q_norm_g[l], k_norm_g[l], tm)
        g1 = norm1_g[l].reshape(1, d)
        n2 = norm2_g[l].reshape(1, d)
        cw = conv_w[l]
        cb, lg, lb = (a[l].reshape(1, -1) for a in (conv_b, conv_ln_g, conv_ln_b))
        woa = w_o[l][:D_ATT].astype(BF16)
        woc = w_o[l][D_ATT:].astype(BF16)
        wq_t = peer_w_query[l].T.astype(BF16)
        sk = peer_sub_keys[l].reshape(PEER_HEADS * 2, PEER_N_KEYS, PEER_HALF)
        skh = sk.astype(BF16)
        skl = (sk - skh.astype(F32)).astype(BF16)
        tab_u = peer_u[l].astype(BF16).reshape(-1, d // LANES, LANES)
        tab_v = peer_v[l].astype(BF16).reshape(-1, d // LANES, LANES)

        sh1, sc1, gt1, sh2, sc2, gt2 = mod_p
        qb, k, v, kb, vb, lf, fc, u = _inproj(xp, sh1, sc1, g1, wts, per_token=False, tm=tm)
        a_out = _prompt_attention(qb, kb, vb, fc)
        c_out = _conv_prompt(u, cw, cb, lg, lb)
        x1, h2 = _tail(xp, a_out, c_out, gt1, sh2, sc2, n2, woa, woc, per_token=False, tm=tm)
        peer = _peer_ffn(h2.reshape(bsz * seq, d), wq_t, skh, skl, tab_u, tab_v)
        xp = _final(x1, gt2, peer.reshape(bsz, seq, d), per_token=False)
        outs[0].append(k.reshape(bsz, seq, N_HEADS, HEAD_DIM))
        outs[1].append(v.reshape(bsz, seq, N_HEADS, HEAD_DIM))
        outs[2].append(lf)
        outs[3].append(u[:, seq - (CONV_WIDTH - 1):, :])

        sh1, sc1, gt1, sh2, sc2, gt2 = mod_s
        qb, k, v, kb, vb, lf, fc, u = _inproj(xs, sh1, sc1, g1, wts, per_token=True, tm=tm)
        shp = (n_dec, s_new, D_ATT)
        a_out = _sample_attention(qb.reshape(shp).astype(F32), k.reshape(shp), v.reshape(shp),
                                  lf.reshape(n_dec, s_new, N_HEADS), cache_k[l], cache_v[l],
                                  cache_logf[l], page_table)
        u_s = u.reshape(shp)
        c_out = _conv_sample(state_conv[l].transpose(1, 0, 2), u_s.transpose(1, 0, 2),
                             cw, cb, lg, lb).transpose(1, 0, 2)
        x1, h2 = _tail(xs, a_out.reshape(1, -1, D_ATT), c_out.reshape(1, -1, D_ATT), gt1, sh2,
                       sc2, n2, woa, woc, per_token=True, tm=tm)
        peer = _peer_ffn(h2.reshape(n_dec * s_new, d), wq_t, skh, skl, tab_u, tab_v)
        xs = _final(x1, gt2, peer.reshape(1, n_dec * s_new, d), per_token=True)
        outs[4].append(k.reshape(n_dec, s_new, N_HEADS, HEAD_DIM))
        outs[5].append(v.reshape(n_dec, s_new, N_HEADS, HEAD_DIM))
        outs[6].append(lf.reshape(n_dec, s_new, N_HEADS))
        outs[7].append(jnp.concatenate([state_conv[l], u_s], axis=1)[:, -(CONV_WIDTH - 1):])
    st = [jnp.stack(o) for o in outs]
    return (xp, xs.reshape(n_dec, s_new, d), st[0], st[1], st[2], st[3], st[4], st[5], st[6],
            st[7])
```

```python
import functools

import jax
import jax.numpy as jnp
from jax import lax
from jax.experimental import pallas as pl
from jax.experimental.pallas import tpu as pltpu

F32 = jnp.float32
BF16 = jnp.bfloat16
I32 = jnp.int32

EPS = 1e-6
N_HEADS = 8
HEAD_DIM = 64
D_ATT = N_HEADS * HEAD_DIM
CONV_WIDTH = 31
PAGE_SIZE = 128
PEER_HEADS = 8
PEER_N_KEYS = 128
PEER_HALF = 128
PEER_TOPK = 16
PEER_SLOTS = PEER_HEADS * PEER_TOPK
LANES = 128
SUBLANES = 8
NEG = -1e30
VMEM_LIMIT = 56 * 1024 * 1024


def _cp(sem, vmem=VMEM_LIMIT):
    return pltpu.CompilerParams(dimension_semantics=sem, vmem_limit_bytes=vmem)


def _dot(a, b):
    return jnp.dot(a, b, preferred_element_type=F32)


def _dot_nt(a, b):
    return lax.dot_general(a, b, (((1,), (1,)), ((), ())), preferred_element_type=F32)


def _split2(a):
    hi = a.astype(BF16)
    lo = (a - hi.astype(F32)).astype(BF16)
    return hi, lo


def _split3(a):
    hi = a.astype(BF16)
    r = a - hi.astype(F32)
    mid = r.astype(BF16)
    lo = (r - mid.astype(F32)).astype(BF16)
    return hi, mid, lo


def _log_sigmoid(x):
    return jnp.minimum(x, 0.0) - jnp.log1p(jnp.exp(-jnp.abs(x)))


def _ada_kernel(c_ref, w_ref, b_ref, o_ref):
    ch, cl = _split2(c_ref[...])
    wh, wl = _split2(w_ref[...])
    o_ref[...] = _dot(ch, wh) + (_dot(ch, wl) + _dot(cl, wh)) + b_ref[...]


def _adaln(c, w_ada, b_ada):
    r, d = c.shape
    n = w_ada.shape[1]
    tn = 512
    return pl.pallas_call(
        _ada_kernel,
        grid=(n // tn,),
        in_specs=[pl.BlockSpec((r, d), lambda j: (0, 0)),
                  pl.BlockSpec((d, tn), lambda j: (0, j)),
                  pl.BlockSpec((1, tn), lambda j: (0, j))],
        out_specs=pl.BlockSpec((r, tn), lambda j: (0, j)),
        out_shape=jax.ShapeDtypeStruct((r, n), F32),
        compiler_params=_cp(("parallel",)),
        name="adaln",
    )(c, w_ada, b_ada.reshape(1, n))


def _inproj_kernel(x_ref, sh_ref, sc_ref, g1_ref, w_ref, wfh_ref, wfl_ref, bf_ref, bglu_ref,
                   qg_ref, kg_ref, ones_ref, tri_ref,
                   q_ref, k_ref, v_ref, kb_ref, vb_ref, lf_ref, fc_ref, u_ref, carry_ref):
    t = pl.program_id(1)

    @pl.when(t == 0)
    def _():
        carry_ref[...] = jnp.zeros_like(carry_ref)

    x = x_ref[...]
    ms = jnp.mean(x * x, axis=-1, keepdims=True)
    h = x * lax.rsqrt(ms + EPS) * g1_ref[...]
    h = h * (1.0 + sc_ref[...]) + sh_ref[...]
    hh, hl = _split2(h)
    z = _dot(hh, w_ref[...])
    fg = _dot(hh, wfh_ref[...]) + (_dot(hh, wfl_ref[...]) + _dot(hl, wfh_ref[...]))
    d = D_ATT
    q = z[:, 0:d]
    k = z[:, d:2 * d]
    v = z[:, 2 * d:3 * d]
    a = z[:, 3 * d:4 * d] + bglu_ref[:, 0:d]
    b = z[:, 4 * d:5 * d] + bglu_ref[:, d:2 * d]
    qs = _dot((q * q).astype(BF16), ones_ref[...])
    ks = _dot((k * k).astype(BF16), ones_ref[...])
    qn = q * lax.rsqrt(qs * (1.0 / HEAD_DIM) + EPS) * qg_ref[...]
    kn = k * lax.rsqrt(ks * (1.0 / HEAD_DIM) + EPS) * kg_ref[...]
    q_ref[...] = (qn * (HEAD_DIM ** -0.5)).astype(BF16)
    k_ref[...] = kn
    kb_ref[...] = kn.astype(BF16)
    v_ref[...] = v
    vb_ref[...] = v.astype(BF16)
    u_ref[...] = a * jax.nn.sigmoid(b)
    lf = _log_sigmoid(fg + bf_ref[...])
    lf_ref[...] = lf[:, 0:N_HEADS]
    l1, l2, l3 = _split3(lf)
    tri = tri_ref[...]
    fc = _dot(tri, l1) + (_dot(tri, l2) + _dot(tri, l3)) + carry_ref[...]
    fc_ref[...] = fc[:, 0:N_HEADS]
    tm = fc.shape[0]
    carry_ref[...] = fc[tm - 1:tm, :]


def _inproj(x, shift, scale, g1, wts, per_token, tm=256):
    bsz, t, d = x.shape
    w, wfh, wfl, bfp, bglu, qg, kg, ones_bd, tri = wts
    row = lambda b, i: (b, i, 0)
    const2 = lambda b, i: (0, 0)
    mod_spec = (pl.BlockSpec((None, tm, d), row) if per_token
                else pl.BlockSpec((None, 1, d), lambda b, i: (b, 0, 0)))
    full = lambda arr: pl.BlockSpec(arr.shape, const2)
    outs = [
        jax.ShapeDtypeStruct((bsz, t, D_ATT), BF16),
        jax.ShapeDtypeStruct((bsz, t, D_ATT), F32),
        jax.ShapeDtypeStruct((bsz, t, D_ATT), F32),
        jax.ShapeDtypeStruct((bsz, t, D_ATT), BF16),
        jax.ShapeDtypeStruct((bsz, t, D_ATT), BF16),
        jax.ShapeDtypeStruct((bsz, t, N_HEADS), F32),
        jax.ShapeDtypeStruct((bsz, t, N_HEADS), F32),
        jax.ShapeDtypeStruct((bsz, t, D_ATT), F32),
    ]
    ospec = lambda s: pl.BlockSpec((None, tm, s.shape[2]), row)
    return pl.pallas_call(
        _inproj_kernel,
        grid=(bsz, t // tm),
        in_specs=[pl.BlockSpec((None, tm, d), row), mod_spec, mod_spec, full(g1), full(w),
                  full(wfh), full(wfl), full(bfp), full(bglu), full(qg), full(kg),
                  full(ones_bd), full(tri)],
        out_specs=[ospec(s) for s in outs],
        out_shape=outs,
        scratch_shapes=[pltpu.VMEM((1, LANES), F32)],
        compiler_params=_cp(("parallel", "arbitrary")),
        name="inproj",
    )(x, shift, scale, g1, w, wfh, wfl, bfp, bglu, qg, kg, ones_bd, tri)


def _attn_kernel(q_ref, k_ref, v_ref, f_ref, o_ref, acc_ref, *, tq, tk):
    qi = pl.program_id(2)
    q = q_ref[...]
    lane = lax.broadcasted_iota(I32, (1, LANES), 1)
    zero = jnp.zeros_like(q)
    qm = [jnp.where(lane < HEAD_DIM, q, zero), jnp.where(lane >= HEAD_DIM, q, zero)]
    acc_ref[...] = jnp.zeros_like(acc_ref)
    rows = lax.broadcasted_iota(I32, (tq, tk), 0)
    cols = lax.broadcasted_iota(I32, (tq, tk), 1)

    def step(j, carry, diag_off=None):
        start = pl.multiple_of(j * tk, tk)
        kc = k_ref[pl.ds(start, tk), :]
        vc = v_ref[pl.ds(start, tk), :]
        fr = f_ref[j]
        new = []
        for hh in range(2):
            m, l = carry[2 * hh], carry[2 * hh + 1]
            s = _dot_nt(qm[hh], kc) - fr[hh:hh + 1, :]
            if diag_off is not None:
                s = jnp.where(cols + diag_off <= rows, s, NEG)
            m_new = jnp.maximum(m, jnp.max(s, axis=1, keepdims=True))
            alpha = jnp.exp(m - m_new)
            p = jnp.exp(s - m_new)
            l_new = alpha * l + jnp.sum(p, axis=1, keepdims=True)
            acc_ref[hh] = alpha * acc_ref[hh] + _dot(p.astype(BF16), vc)
            new += [m_new, l_new]
        return tuple(new)

    init = tuple(jnp.full((tq, 1), NEG, F32) if i % 2 == 0 else jnp.zeros((tq, 1), F32)
                 for i in range(4))
    n_diag = tq // tk
    carry = lax.fori_loop(0, qi * n_diag, step, init)
    for dd in range(n_diag):
        carry = step(qi * n_diag + dd, carry, diag_off=dd * tk)
    o0 = acc_ref[0] / carry[1]
    o1 = acc_ref[1] / carry[3]
    o_ref[...] = jnp.where(lane < HEAD_DIM, o0, o1).astype(o_ref.dtype)


def _prompt_attention(qb, kb, vb, fcum, tq=512, tk=256):
    bsz, t, _ = qb.shape
    n_pairs = N_HEADS // 2
    nk = t // tk
    f_rows = fcum.transpose(0, 2, 1).reshape(bsz, n_pairs, 2, nk, tk).transpose(0, 1, 3, 2, 4)
    return pl.pallas_call(
        functools.partial(_attn_kernel, tq=tq, tk=tk),
        grid=(bsz, n_pairs, t // tq),
        in_specs=[pl.BlockSpec((None, tq, LANES), lambda b, p, i: (b, i, p)),
                  pl.BlockSpec((None, t, LANES), lambda b, p, i: (b, 0, p)),
                  pl.BlockSpec((None, t, LANES), lambda b, p, i: (b, 0, p)),
                  pl.BlockSpec((None, None, nk, 2, tk), lambda b, p, i: (b, p, 0, 0, 0))],
        out_specs=pl.BlockSpec((None, tq, LANES), lambda b, p, i: (b, i, p)),
        out_shape=jax.ShapeDtypeStruct((bsz, t, D_ATT), BF16),
        scratch_shapes=[pltpu.VMEM((2, tq, LANES), F32)],
        compiler_params=_cp(("parallel", "parallel", "arbitrary")),
        name="prompt_attn",
    )(qb, kb, vb, f_rows)


def _sattn_kernel(pt_ref, q_ref, *refs, n_new, n_pg):
    ck = refs[0:n_pg]
    cv = refs[n_pg:2 * n_pg]
    clf = refs[2 * n_pg:3 * n_pg]
    kn_ref, vn_ref, lfn_ref, triu_ref, o_ref, m_ref, l_ref, acc_ref, fcar_ref = refs[3 * n_pg:]
    p = pl.program_id(1)
    nq = n_new * N_HEADS
    lane = lax.broadcasted_iota(I32, (N_HEADS, D_ATT), 1)
    hrow = lax.broadcasted_iota(I32, (N_HEADS, D_ATT), 0)
    hmask = (lane // HEAD_DIM == hrow)

    @pl.when(p == 0)
    def _():
        m_ref[...] = jnp.full_like(m_ref, NEG)
        l_ref[...] = jnp.zeros_like(l_ref)
        acc_ref[...] = jnp.zeros_like(acc_ref)
        fcar_ref[...] = jnp.zeros_like(fcar_ref)

    q = q_ref[...]
    zq = jnp.zeros((N_HEADS, D_ATT), F32)
    qexp = jnp.concatenate(
        [jnp.where(hmask, jnp.broadcast_to(q[t:t + 1, :], (N_HEADS, D_ATT)), zq)
         for t in range(n_new)], axis=0).astype(BF16)

    def update(scores, values, transposed):
        m = m_ref[...]
        m_new = m
        for s in scores:
            m_new = jnp.maximum(m_new, jnp.max(s, axis=1, keepdims=True))
        alpha = jnp.exp(m - m_new)
        l_new = alpha * l_ref[...]
        acc = alpha * acc_ref[...]
        for s, vmat in zip(scores, values):
            pr = jnp.exp(s - m_new)
            l_new = l_new + jnp.sum(pr, axis=1, keepdims=True)
            pb = pr.astype(BF16)
            acc = acc + (_dot_nt(pb, vmat) if transposed else _dot(pb, vmat))
        l_ref[...] = l_new
        acc_ref[...] = acc
        m_ref[...] = m_new

    l1, l2, l3 = _split3(jnp.concatenate([r[...] for r in clf], axis=0))
    tri = triu_ref[...]
    local = _dot(l1, tri) + (_dot(l2, tri) + _dot(l3, tri))
    fcar = fcar_ref[...]
    scores = []
    for g in range(n_pg):
        fp = local[g * N_HEADS:(g + 1) * N_HEADS, :] + fcar
        fcar = jnp.broadcast_to(fp[:, PAGE_SIZE - 1:PAGE_SIZE], fcar.shape)
        scores.append(_dot(qexp, ck[g][...].astype(BF16))
                      - jnp.concatenate([fp] * n_new, axis=0))
    fcar_ref[...] = fcar
    update(scores, [cv[g][...].astype(BF16) for g in range(n_pg)], transposed=True)

    @pl.when(p == pl.num_programs(1) - 1)
    def _():
        lfn = lfn_ref[...]
        cols = []
        run = fcar[:, 0:1]
        for t in range(n_new):
            run = run + lfn[:, t:t + 1]
            cols.append(run)
        pad = SUBLANES - n_new
        fnew = jnp.concatenate(cols + [jnp.zeros((N_HEADS, pad), F32)], axis=1)
        zk = jnp.zeros((pad, D_ATT), F32)
        kn = jnp.concatenate([kn_ref[...], zk], axis=0).astype(BF16)
        vn = jnp.concatenate([vn_ref[...], zk], axis=0).astype(BF16)
        s2 = _dot_nt(qexp, kn) - jnp.concatenate([fnew] * n_new, axis=0)
        r = lax.broadcasted_iota(I32, (nq, SUBLANES), 0) // N_HEADS
        c = lax.broadcasted_iota(I32, (nq, SUBLANES), 1)
        update([jnp.where(c <= r, s2, NEG)], [vn], transposed=False)
        out = acc_ref[...] / l_ref[...]
        zo = jnp.zeros((N_HEADS, D_ATT), F32)
        rows = [jnp.sum(jnp.where(hmask, out[t * N_HEADS:(t + 1) * N_HEADS, :], zo),
                        axis=0, keepdims=True) for t in range(n_new)]
        o_ref[...] = jnp.concatenate(rows, axis=0)


def _sample_attention(qb, kn, vn, lfn, cache_k, cache_v, cache_logf, page_table, n_pg=4):
    n, s_new, _ = qb.shape
    n_pages = page_table.shape[1]
    n_phys = cache_k.shape[0]
    nq = s_new * N_HEADS
    ck = cache_k.transpose(0, 2, 3, 1).reshape(n_phys, D_ATT, PAGE_SIZE)
    cv = cache_v.transpose(0, 2, 3, 1).reshape(n_phys, D_ATT, PAGE_SIZE)
    clf = cache_logf.transpose(0, 2, 1)
    lfn_t = lfn.transpose(0, 2, 1)
    triu = jnp.triu(jnp.ones((PAGE_SIZE, PAGE_SIZE), BF16))
    seq = lambda i, p, pt: (i, 0, 0)
    tok_spec = pl.BlockSpec((None, s_new, D_ATT), seq)

    def page(g):
        return lambda i, p, pt: (pt[i * n_pages + p * n_pg + g], 0, 0)

    kv_specs = [pl.BlockSpec((None, D_ATT, PAGE_SIZE), page(g)) for g in range(n_pg)]
    lf_specs = [pl.BlockSpec((None, N_HEADS, PAGE_SIZE), page(g)) for g in range(n_pg)]
    gs = pltpu.PrefetchScalarGridSpec(
        num_scalar_prefetch=1,
        grid=(n, n_pages // n_pg),
        in_specs=[tok_spec] + kv_specs + kv_specs + lf_specs + [
            tok_spec, tok_spec,
            pl.BlockSpec((None, N_HEADS, s_new), seq),
            pl.BlockSpec(triu.shape, lambda i, p, pt: (0, 0))],
        out_specs=tok_spec,
        scratch_shapes=[pltpu.VMEM((nq, 1), F32), pltpu.VMEM((nq, 1), F32),
                        pltpu.VMEM((nq, D_ATT), F32),
                        pltpu.VMEM((N_HEADS, PAGE_SIZE), F32)],
    )
    return pl.pallas_call(
        functools.partial(_sattn_kernel, n_new=s_new, n_pg=n_pg),
        grid_spec=gs,
        out_shape=jax.ShapeDtypeStruct((n, s_new, D_ATT), F32),
        compiler_params=_cp(("parallel", "arbitrary")),
        name="sample_attn",
    )(page_table.reshape(-1), qb, *([ck] * n_pg), *([cv] * n_pg), *([clf] * n_pg),
      kn, vn, lfn_t, triu)


def _ln_swish(y, g, b):
    mu = jnp.mean(y, axis=-1, keepdims=True)
    yc = y - mu
    var = jnp.mean(yc * yc, axis=-1, keepdims=True)
    yn = yc * lax.rsqrt(var + EPS) * g + b
    return yn * jax.nn.sigmoid(yn)


def _conv_kernel(u_ref, cw_ref, cb_ref, g_ref, b_ref, o_ref, ext_ref, *, tm, hist):
    t = pl.program_id(1)

    @pl.when(t == 0)
    def _():
        ext_ref[0:hist, :] = jnp.zeros((hist, ext_ref.shape[1]), F32)

    ext_ref[hist:hist + tm, :] = u_ref[...]
    off = hist - (CONV_WIDTH - 1)
    y = jnp.zeros(u_ref.shape, F32) + cb_ref[...]
    for w in range(CONV_WIDTH):
        y = y + ext_ref[off + w:off + w + tm, :] * cw_ref[w:w + 1, :]
    o_ref[...] = _ln_swish(y, g_ref[...], b_ref[...]).astype(o_ref.dtype)
    ext_ref[0:hist, :] = ext_ref[tm:tm + hist, :]


def _conv_prompt(u, cw, cb, g, b, tm=512):
    bsz, t, c = u.shape
    hist = 32
    row = lambda bb, i: (bb, i, 0)
    const = lambda bb, i: (0, 0)
    return pl.pallas_call(
        functools.partial(_conv_kernel, tm=tm, hist=hist),
        grid=(bsz, t // tm),
        in_specs=[pl.BlockSpec((None, tm, c), row), pl.BlockSpec(cw.shape, const),
                  pl.BlockSpec((1, c), const), pl.BlockSpec((1, c), const),
                  pl.BlockSpec((1, c), const)],
        out_specs=pl.BlockSpec((None, tm, c), row),
        out_shape=jax.ShapeDtypeStruct((bsz, t, c), BF16),
        scratch_shapes=[pltpu.VMEM((hist + tm, c), F32)],
        compiler_params=_cp(("parallel", "arbitrary")),
        name="conv_prompt",
    )(u, cw, cb, g, b)


def _conv_step_kernel(st_ref, u_ref, cw_ref, cb_ref, g_ref, b_ref, o_ref, *, n_hist, n_new):
    def ext(i):
        return st_ref[i] if i < n_hist else u_ref[i - n_hist]

    for t in range(n_new):
        y = jnp.zeros(o_ref.shape[1:], F32) + cb_ref[...]
        for w in range(CONV_WIDTH):
            y = y + ext(t + w) * cw_ref[w:w + 1, :]
        o_ref[t] = _ln_swish(y, g_ref[...], b_ref[...]).astype(o_ref.dtype)


def _conv_sample(state_t, u_t, cw, cb, g, b):
    n_hist, n, c = state_t.shape
    n_new = u_t.shape[0]
    return pl.pallas_call(
        functools.partial(_conv_step_kernel, n_hist=n_hist, n_new=n_new),
        out_shape=jax.ShapeDtypeStruct((n_new, n, c), BF16),
        compiler_params=pltpu.CompilerParams(vmem_limit_bytes=VMEM_LIMIT),
        name="conv_sample",
    )(state_t, u_t, cw, cb, g, b)


def _tail_kernel(x_ref, a_ref, c_ref, g1_ref, sh_ref, sc_ref, n2_ref, woa_ref, woc_ref,
                 x1_ref, h2_ref):
    proj = (_dot(a_ref[...].astype(BF16), woa_ref[...])
            + _dot(c_ref[...].astype(BF16), woc_ref[...]))
    x1 = x_ref[...] + g1_ref[...] * proj
    x1_ref[...] = x1
    ms = jnp.mean(x1 * x1, axis=-1, keepdims=True)
    h2 = x1 * lax.rsqrt(ms + EPS) * n2_ref[...]
    h2_ref[...] = h2 * (1.0 + sc_ref[...]) + sh_ref[...]


def _tail(x, attn, conv, gate1, shift2, scale2, n2g, woa, woc, per_token, tm=256):
    bsz, t, d = x.shape
    row = lambda b, i: (b, i, 0)
    const2 = lambda b, i: (0, 0)
    mod_spec = (pl.BlockSpec((None, tm, d), row) if per_token
                else pl.BlockSpec((None, 1, d), lambda b, i: (b, 0, 0)))
    half = pl.BlockSpec((None, tm, D_ATT), row)
    full = pl.BlockSpec((None, tm, d), row)
    return pl.pallas_call(
        _tail_kernel,
        grid=(bsz, t // tm),
        in_specs=[full, half, half, mod_spec, mod_spec, mod_spec,
                  pl.BlockSpec(n2g.shape, const2), pl.BlockSpec(woa.shape, const2),
                  pl.BlockSpec(woc.shape, const2)],
        out_specs=[full, full],
        out_shape=[jax.ShapeDtypeStruct((bsz, t, d), F32)] * 2,
        compiler_params=_cp(("parallel", "parallel")),
        name="tail",
    )(x, attn, conv, gate1, shift2, scale2, n2g, woa, woc)


def _top_rows(s, order, payload, k):
    big = float(2 ** 24)
    vals, outs = [], []
    for _ in range(k):
        m = jnp.max(s, axis=0, keepdims=True)
        pos = jnp.min(jnp.where(s == m, order, big), axis=0, keepdims=True)
        hit = order == pos
        vals.append(m)
        if payload is None:
            outs.append(pos)
        else:
            outs.append(jnp.sum(jnp.where(hit, payload, 0), axis=0, keepdims=True))
        s = jnp.where(hit, -jnp.inf, s)
    return jnp.concatenate(vals, axis=0), jnp.concatenate(outs, axis=0)


def _pair_groups():
    groups = [(0, PEER_TOPK, PEER_TOPK)]
    for a in range(1, SUBLANES):
        groups.append((a, SUBLANES, PEER_TOPK // (a + 1)))
    groups.append((None, SUBLANES, SUBLANES))
    return groups


def _route_kernel(h_ref, wq_ref, skh_ref, skl_ref, e_ref, g_ref):
    hb = h_ref[...].astype(BF16)
    tb = hb.shape[0]
    key_order = lax.broadcasted_iota(I32, (PEER_N_KEYS, tb), 0).astype(F32)
    groups = _pair_groups()
    orders, masks = [], []
    for a, rows, valid in groups:
        r = lax.broadcasted_iota(I32, (rows, tb), 0)
        flat = (a * PEER_TOPK + r) if a is not None else (SUBLANES + r) * PEER_TOPK
        orders.append(flat.astype(F32))
        masks.append(r < valid)
    pair_order = jnp.concatenate(orders, axis=0)
    pair_valid = jnp.concatenate(masks, axis=0)
    e_rows, g_rows = [], []
    for hd in range(PEER_HEADS):
        tops = []
        for half in range(2):
            i = hd * 2 + half
            wq = wq_ref[i * PEER_HALF:(i + 1) * PEER_HALF, :]
            qt = _dot_nt(wq, hb)
            qh, ql = _split2(qt)
            skh = skh_ref[i]
            s = _dot(skh, qh) + (_dot(skh, ql) + _dot(skl_ref[i], qh))
            vals, pos = _top_rows(s, key_order, None, PEER_TOPK)
            tops.append((vals, pos.astype(I32)))
        (s1, i1), (s2, i2) = tops
        cand, cidx = [], []
        for a, rows, _ in groups:
            if a is not None:
                cand.append(s1[a:a + 1, :] + s2[0:rows, :])
                cidx.append(i1[a:a + 1, :] * PEER_N_KEYS + i2[0:rows, :])
            else:
                cand.append(s1[SUBLANES:, :] + s2[0:1, :])
                cidx.append(i1[SUBLANES:, :] * PEER_N_KEYS + i2[0:1, :])
        cand = jnp.where(pair_valid, jnp.concatenate(cand, axis=0), -jnp.inf)
        top, eidx = _top_rows(cand, pair_order, jnp.concatenate(cidx, axis=0), PEER_TOPK)
        ex = jnp.exp(top - top[0:1, :])
        g_rows.append(ex / jnp.sum(ex, axis=0, keepdims=True))
        e_rows.append(eidx)
    e_ref[...] = jnp.concatenate(e_rows, axis=0).T
    g_ref[...] = jnp.concatenate(g_rows, axis=0).T


def _route(h2, wq_t, skh, skl, tb=128):
    n, d = h2.shape
    const2 = lambda i: (0, 0)
    const3 = lambda i: (0, 0, 0)
    return pl.pallas_call(
        _route_kernel,
        grid=(n // tb,),
        in_specs=[pl.BlockSpec((tb, d), lambda i: (i, 0)), pl.BlockSpec(wq_t.shape, const2),
                  pl.BlockSpec(skh.shape, const3), pl.BlockSpec(skl.shape, const3)],
        out_specs=[pl.BlockSpec((tb, PEER_SLOTS), lambda i: (i, 0))] * 2,
        out_shape=[jax.ShapeDtypeStruct((n, PEER_SLOTS), I32),
                   jax.ShapeDtypeStruct((n, PEER_SLOTS), F32)],
        compiler_params=_cp(("parallel",)),
        name="peer_route",
    )(h2, wq_t, skh, skl)


def _peer_act_kernel(idx_ref, tab_ref, h_ref, g_ref, c_ref, ps0_ref, ps1_ref, act_ref, *, tb):
    def gather(t, ps_ref):
        hv = h_ref[t]
        idx_t = idx_ref.at[pl.ds(t * PEER_SLOTS, PEER_SLOTS)]
        for j in range(PEER_SLOTS):
            ps_ref[j * SUBLANES:(j + 1) * SUBLANES, :] = tab_ref[idx_t[j]].astype(F32) * hv

    def reduce(t, ps_ref):
        s = ps_ref[pl.ds(0, PEER_SLOTS, stride=SUBLANES), :]
        for r in range(1, SUBLANES):
            s = s + ps_ref[pl.ds(r, PEER_SLOTS, stride=SUBLANES), :]
        act_ref[pl.ds(t, 1), :] = jnp.sum(s.T, axis=0, keepdims=True)

    def pair(i, carry):
        gather(2 * i, ps0_ref)
        reduce(jnp.maximum(2 * i - 1, 0), ps1_ref)
        gather(2 * i + 1, ps1_ref)
        reduce(2 * i, ps0_ref)
        return carry

    ps1_ref[...] = jnp.zeros_like(ps1_ref)
    lax.fori_loop(0, tb // 2, pair, 0)
    reduce(tb - 1, ps1_ref)
    act = act_ref[...]
    gelu = 0.5 * act * (1.0 + lax.erf(act * (2.0 ** -0.5)))
    c_ref[...] = g_ref[...] * gelu


def _peer_act(eidx, tab_u, h2, g, tb=64):
    n = eidx.shape[0]
    d = h2.shape[1]
    hc = h2.reshape(n, d // LANES, LANES)
    return pl.pallas_call(
        functools.partial(_peer_act_kernel, tb=tb),
        grid=(n // tb,),
        in_specs=[pl.BlockSpec((tb * PEER_SLOTS,), lambda i: (i,), memory_space=pltpu.SMEM),
                  pl.BlockSpec(memory_space=pltpu.VMEM),
                  pl.BlockSpec((tb, d // LANES, LANES), lambda i: (i, 0, 0)),
                  pl.BlockSpec((tb, PEER_SLOTS), lambda i: (i, 0))],
        out_specs=pl.BlockSpec((tb, PEER_SLOTS), lambda i: (i, 0)),
        out_shape=jax.ShapeDtypeStruct((n, PEER_SLOTS), F32),
        scratch_shapes=[pltpu.VMEM((PEER_SLOTS * SUBLANES, LANES), F32),
                        pltpu.VMEM((PEER_SLOTS * SUBLANES, LANES), F32),
                        pltpu.VMEM((tb, PEER_SLOTS), F32)],
        compiler_params=_cp(("arbitrary",)),
        name="peer_act",
    )(eidx.reshape(-1), tab_u, hc, g)


def _peer_out_kernel(idx_ref, tab_ref, c_ref, o_ref, cb_ref, *, tb):
    n_acc = 4

    def tok(t, carry):
        idx_t = idx_ref.at[pl.ds(t * PEER_SLOTS, PEER_SLOTS)]
        crow = c_ref[pl.ds(t, 1), :]
        cb_ref[...] = jnp.broadcast_to(crow, (PEER_SLOTS, PEER_SLOTS)).T
        accs = [jnp.zeros((SUBLANES, LANES), F32) for _ in range(n_acc)]
        for j in range(PEER_SLOTS):
            accs[j % n_acc] = (accs[j % n_acc]
                               + cb_ref[j:j + 1, :] * tab_ref[idx_t[j]].astype(F32))
        o_ref[t] = (accs[0] + accs[1]) + (accs[2] + accs[3])
        return carry

    lax.fori_loop(0, tb, tok, 0)


def _peer_out(eidx, coef, tab_v, tb=64):
    n = eidx.shape[0]
    chunks = tab_v.shape[1]
    out = pl.pallas_call(
        functools.partial(_peer_out_kernel, tb=tb),
        grid=(n // tb,),
        in_specs=[pl.BlockSpec((tb * PEER_SLOTS,), lambda i: (i,), memory_space=pltpu.SMEM),
                  pl.BlockSpec(memory_space=pltpu.VMEM),
                  pl.BlockSpec((tb, PEER_SLOTS), lambda i: (i, 0))],
        out_specs=pl.BlockSpec((tb, chunks, LANES), lambda i: (i, 0, 0)),
        out_shape=jax.ShapeDtypeStruct((n, chunks, LANES), F32),
        scratch_shapes=[pltpu.VMEM((PEER_SLOTS, PEER_SLOTS), F32)],
        compiler_params=_cp(("arbitrary",)),
        name="peer_out",
    )(eidx.reshape(-1), tab_v, coef)
    return out.reshape(n, chunks * LANES)


def _final_kernel(x1_ref, g2_ref, p_ref, y_ref):
    y_ref[...] = x1_ref[...] + g2_ref[...] * p_ref[...]


def _final(x1, gate2, peer, per_token, tm=512):
    bsz, t, d = x1.shape
    row = lambda b, i: (b, i, 0)
    mod_spec = (pl.BlockSpec((None, tm, d), row) if per_token
                else pl.BlockSpec((None, 1, d), lambda b, i: (b, 0, 0)))
    full = pl.BlockSpec((None, tm, d), row)
    return pl.pallas_call(
        _final_kernel,
        grid=(bsz, t // tm),
        in_specs=[full, mod_spec, full],
        out_specs=full,
        out_shape=jax.ShapeDtypeStruct((bsz, t, d), F32),
        compiler_params=_cp(("parallel", "parallel")),
        name="final",
    )(x1, gate2, peer)


def _prep_inproj(w_in, b_f, b_glu, q_norm_g, k_norm_g, tm):
    d = D_ATT
    nf = 3 * d + N_HEADS
    w = jnp.concatenate([w_in[:, :3 * d], w_in[:, nf:]], axis=1).astype(BF16)
    wf = jnp.pad(w_in[:, 3 * d:nf], ((0, 0), (0, LANES - N_HEADS)))
    wfh = wf.astype(BF16)
    wfl = (wf - wfh.astype(F32)).astype(BF16)
    bfp = jnp.pad(b_f, (0, LANES - N_HEADS)).reshape(1, LANES)
    qg = jnp.tile(q_norm_g, N_HEADS).reshape(1, d)
    kg = jnp.tile(k_norm_g, N_HEADS).reshape(1, d)
    hid = jnp.arange(d) // HEAD_DIM
    ones_bd = (hid[:, None] == hid[None, :]).astype(BF16)
    tri = jnp.tril(jnp.ones((tm, tm), BF16))
    return (w, wfh, wfl, bfp, b_glu.reshape(1, -1), qg, kg, ones_bd, tri)


def _peer_ffn(h2, wq_t, skh, skl, tab_u, tab_v):
    eidx, g = _route(h2, wq_t, skh, skl)
    coef = _peer_act(eidx, tab_u, h2, g)
    return _peer_out(eidx, coef, tab_v)


def kernel(x_prompt, x_sample, c_prompt, c_sample, cache_k, cache_v, cache_logf, state_conv,
           page_table, norm1_g, norm2_g, w_ada, b_ada, w_in, b_f, b_glu, q_norm_g, k_norm_g,
           conv_w, conv_b, conv_ln_g, conv_ln_b, w_o, peer_w_query, peer_sub_keys, peer_u,
           peer_v):
    depth = w_ada.shape[0]
    bsz, seq, d = x_prompt.shape
    n_dec, s_new, _ = x_sample.shape
    tm = 256
    xp, xs = x_prompt, x_sample.reshape(1, n_dec * s_new, d)
    outs = [[] for _ in range(8)]
    for l in range(depth):
        n_c = bsz + n_dec
        pad = (-n_c) % SUBLANES
        c_all = jnp.pad(jnp.concatenate([c_prompt, c_sample], axis=0), ((0, pad), (0, 0)))
        mod = _adaln(c_all, w_ada[l], b_ada[l])
        mod_p = [m.reshape(bsz, 1, d) for m in jnp.split(mod[:bsz], 6, axis=-1)]
        mod_s = [jnp.repeat(m, s_new, axis=0).reshape(1, n_dec * s_new, d)
                 for m in jnp.split(mod[bsz:n_c], 6, axis=-1)]

        wts = _prep_inproj(w_in[l], b_f[l], b_glu[l], q_norm_g[l], k_norm_g[l], tm)
        g1 = norm1_g[l].reshape(1, d)
        n2 = norm2_g[l].reshape(1, d)
        cw = conv_w[l]
        cb, lg, lb = (a[l].reshape(1, -1) for a in (conv_b, conv_ln_g, conv_ln_b))
        woa = w_o[l][:D_ATT].astype(BF16)
        woc = w_o[l][D_ATT:].astype(BF16)
        wq_t = peer_w_query[l].T.astype(BF16)
        sk = peer_sub_keys[l].reshape(PEER_HEADS * 2, PEER_N_KEYS, PEER_HALF)
        skh = sk.astype(BF16)
        skl = (sk - skh.astype(F32)).astype(BF16)
        tab_u = peer_u[l].astype(BF16).reshape(-1, d // LANES, LANES)
        tab_v = peer_v[l].astype(BF16).reshape(-1, d // LANES, LANES)

        sh1, sc1, gt1, sh2, sc2, gt2 = mod_p
        qb, k, v, kb, vb, lf, fc, u = _inproj(xp, sh1, sc1, g1, wts, per_token=False, tm=tm)
        a_out = _prompt_attention(qb, kb, vb, fc)
        c_out = _conv_prompt(u, cw, cb, lg, lb)
        x1, h2 = _tail(xp, a_out, c_out, gt1, sh2, sc2, n2, woa, woc, per_token=False, tm=tm)
        peer = _peer_ffn(h2.reshape(bsz * seq, d), wq_t, skh, skl, tab_u, tab_v)
        xp = _final(x1, gt2, peer.reshape(bsz, seq, d), per_token=False)
        outs[0].append(k.reshape(bsz, seq, N_HEADS, HEAD_DIM))
        outs[1].append(v.reshape(bsz, seq, N_HEADS, HEAD_DIM))
        outs[2].append(lf)
        outs[3].append(u[:, seq - (CONV_WIDTH - 1):, :])

        sh1, sc1, gt1, sh2, sc2, gt2 = mod_s
        qb, k, v, kb, vb, lf, fc, u = _inproj(xs, sh1, sc1, g1, wts, per_token=True, tm=tm)
        shp = (n_dec, s_new, D_ATT)
        a_out = _sample_attention(qb.reshape(shp).astype(F32), k.reshape(shp), v.reshape(shp),
                                  lf.reshape(n_dec, s_new, N_HEADS), cache_k[l], cache_v[l],
                                  cache_logf[l], page_table)
        u_s = u.reshape(shp)
        c_out = _conv_sample(state_conv[l].transpose(1, 0, 2), u_s.transpose(1, 0, 2),
                             cw, cb, lg, lb).transpose(1, 0, 2)
        x1, h2 = _tail(xs, a_out.reshape(1, -1, D_ATT), c_out.reshape(1, -1, D_ATT), gt1, sh2,
                       sc2, n2, woa, woc, per_token=True, tm=tm)
        peer = _peer_ffn(h2.reshape(n_dec * s_new, d), wq_t, skh, skl, tab_u, tab_v)
        xs = _final(x1, gt2, peer.reshape(1, n_dec * s_new, d), per_token=True)
        outs[4].append(k.reshape(n_dec, s_new, N_HEADS, HEAD_DIM))
        outs[5].append(v.reshape(n_dec, s_new, N_HEADS, HEAD_DIM))
        outs[6].append(lf.reshape(n_dec, s_new, N_HEADS))
        outs[7].append(jnp.concatenate([state_conv[l], u_s], axis=1)[:, -(CONV_WIDTH - 1):])
    st = [jnp.stack(o) for o in outs]
    return (xp, xs.reshape(n_dec, s_new, d), st[0], st[1], st[2], st[3], st[4], st[5], st[6],
            st[7])
```

```python
import functools

import jax
import jax.numpy as jnp
from jax import lax
from jax.experimental import pallas as pl
from jax.experimental.pallas import tpu as pltpu

F32 = jnp.float32
BF16 = jnp.bfloat16
I32 = jnp.int32

EPS = 1e-6
N_HEADS = 8
HEAD_DIM = 64
D_ATT = N_HEADS * HEAD_DIM
CONV_WIDTH = 31
PAGE_SIZE = 128
PEER_HEADS = 8
PEER_N_KEYS = 128
PEER_HALF = 128
PEER_TOPK = 16
PEER_SLOTS = PEER_HEADS * PEER_TOPK
LANES = 128
SUBLANES = 8
NEG = -1e30
VMEM_LIMIT = 56 * 1024 * 1024


def _cp(sem, vmem=VMEM_LIMIT):
    return pltpu.CompilerParams(dimension_semantics=sem, vmem_limit_bytes=vmem)


def _dot(a, b):
    return jnp.dot(a, b, preferred_element_type=F32)


def _dot_nt(a, b):
    return lax.dot_general(a, b, (((1,), (1,)), ((), ())), preferred_element_type=F32)


def _split2(a):
    hi = a.astype(BF16)
    lo = (a - hi.astype(F32)).astype(BF16)
    return hi, lo


def _split3(a):
    hi = a.astype(BF16)
    r = a - hi.astype(F32)
    mid = r.astype(BF16)
    lo = (r - mid.astype(F32)).astype(BF16)
    return hi, mid, lo


def _log_sigmoid(x):
    return jnp.minimum(x, 0.0) - jnp.log1p(jnp.exp(-jnp.abs(x)))


def _ada_kernel(c_ref, w_ref, b_ref, o_ref):
    ch, cl = _split2(c_ref[...])
    wh, wl = _split2(w_ref[...])
    o_ref[...] = _dot(ch, wh) + (_dot(ch, wl) + _dot(cl, wh)) + b_ref[...]


def _adaln(c, w_ada, b_ada):
    r, d = c.shape
    n = w_ada.shape[1]
    tn = 512
    return pl.pallas_call(
        _ada_kernel,
        grid=(n // tn,),
        in_specs=[pl.BlockSpec((r, d), lambda j: (0, 0)),
                  pl.BlockSpec((d, tn), lambda j: (0, j)),
                  pl.BlockSpec((1, tn), lambda j: (0, j))],
        out_specs=pl.BlockSpec((r, tn), lambda j: (0, j)),
        out_shape=jax.ShapeDtypeStruct((r, n), F32),
        compiler_params=_cp(("parallel",)),
        name="adaln",
    )(c, w_ada, b_ada.reshape(1, n))


def _inproj_kernel(x_ref, sh_ref, sc_ref, g1_ref, w_ref, wfh_ref, wfl_ref, bf_ref, bglu_ref,
                   qg_ref, kg_ref, ones_ref, tri_ref,
                   q_ref, k_ref, v_ref, kb_ref, vb_ref, lf_ref, fc_ref, u_ref, carry_ref):
    t = pl.program_id(1)

    @pl.when(t == 0)
    def _():
        carry_ref[...] = jnp.zeros_like(carry_ref)

    x = x_ref[...]
    ms = jnp.mean(x * x, axis=-1, keepdims=True)
    h = x * lax.rsqrt(ms + EPS) * g1_ref[...]
    h = h * (1.0 + sc_ref[...]) + sh_ref[...]
    hh, hl = _split2(h)
    z = _dot(hh, w_ref[...])
    fg = _dot(hh, wfh_ref[...]) + (_dot(hh, wfl_ref[...]) + _dot(hl, wfh_ref[...]))
    d = D_ATT
    q = z[:, 0:d]
    k = z[:, d:2 * d]
    v = z[:, 2 * d:3 * d]
    a = z[:, 3 * d:4 * d] + bglu_ref[:, 0:d]
    b = z[:, 4 * d:5 * d] + bglu_ref[:, d:2 * d]
    qs = _dot((q * q).astype(BF16), ones_ref[...])
    ks = _dot((k * k).astype(BF16), ones_ref[...])
    qn = q * lax.rsqrt(qs * (1.0 / HEAD_DIM) + EPS) * qg_ref[...]
    kn = k * lax.rsqrt(ks * (1.0 / HEAD_DIM) + EPS) * kg_ref[...]
    q_ref[...] = (qn * (HEAD_DIM ** -0.5)).astype(BF16)
    k_ref[...] = kn
    kb_ref[...] = kn.astype(BF16)
    v_ref[...] = v
    vb_ref[...] = v.astype(BF16)
    u_ref[...] = a * jax.nn.sigmoid(b)
    lf = _log_sigmoid(fg + bf_ref[...])
    lf_ref[...] = lf[:, 0:N_HEADS]
    l1, l2, l3 = _split3(lf)
    tri = tri_ref[...]
    fc = _dot(tri, l1) + (_dot(tri, l2) + _dot(tri, l3)) + carry_ref[...]
    fc_ref[...] = fc[:, 0:N_HEADS]
    tm = fc.shape[0]
    carry_ref[...] = fc[tm - 1:tm, :]


def _inproj(x, shift, scale, g1, wts, per_token, tm=256):
    bsz, t, d = x.shape
    w, wfh, wfl, bfp, bglu, qg, kg, ones_bd, tri = wts
    row = lambda b, i: (b, i, 0)
    const2 = lambda b, i: (0, 0)
    mod_spec = (pl.BlockSpec((None, tm, d), row) if per_token
                else pl.BlockSpec((None, 1, d), lambda b, i: (b, 0, 0)))
    full = lambda arr: pl.BlockSpec(arr.shape, const2)
    outs = [
        jax.ShapeDtypeStruct((bsz, t, D_ATT), BF16),
        jax.ShapeDtypeStruct((bsz, t, D_ATT), F32),
        jax.ShapeDtypeStruct((bsz, t, D_ATT), F32),
        jax.ShapeDtypeStruct((bsz, t, D_ATT), BF16),
        jax.ShapeDtypeStruct((bsz, t, D_ATT), BF16),
        jax.ShapeDtypeStruct((bsz, t, N_HEADS), F32),
        jax.ShapeDtypeStruct((bsz, t, N_HEADS), F32),
        jax.ShapeDtypeStruct((bsz, t, D_ATT), F32),
    ]
    ospec = lambda s: pl.BlockSpec((None, tm, s.shape[2]), row)
    return pl.pallas_call(
        _inproj_kernel,
        grid=(bsz, t // tm),
        in_specs=[pl.BlockSpec((None, tm, d), row), mod_spec, mod_spec, full(g1), full(w),
                  full(wfh), full(wfl), full(bfp), full(bglu), full(qg), full(kg),
                  full(ones_bd), full(tri)],
        out_specs=[ospec(s) for s in outs],
        out_shape=outs,
        scratch_shapes=[pltpu.VMEM((1, LANES), F32)],
        compiler_params=_cp(("parallel", "arbitrary")),
        name="inproj",
    )(x, shift, scale, g1, w, wfh, wfl, bfp, bglu, qg, kg, ones_bd, tri)


def _attn_kernel(q_ref, k_ref, v_ref, f_ref, o_ref, acc_ref, *, tq, tk):
    qi = pl.program_id(2)
    q = q_ref[...]
    lane = lax.broadcasted_iota(I32, (1, LANES), 1)
    zero = jnp.zeros_like(q)
    qm = [jnp.where(lane < HEAD_DIM, q, zero), jnp.where(lane >= HEAD_DIM, q, zero)]
    acc_ref[...] = jnp.zeros_like(acc_ref)
    rows = lax.broadcasted_iota(I32, (tq, tk), 0)
    cols = lax.broadcasted_iota(I32, (tq, tk), 1)

    def step(j, carry, diag_off=None):
        start = pl.multiple_of(j * tk, tk)
        kc = k_ref[pl.ds(start, tk), :]
        vc = v_ref[pl.ds(start, tk), :]
        fr = f_ref[j]
        new = []
        for hh in range(2):
            m, l = carry[2 * hh], carry[2 * hh + 1]
            s = _dot_nt(qm[hh], kc) - fr[hh:hh + 1, :]
            if diag_off is not None:
                s = jnp.where(cols + diag_off <= rows, s, NEG)
            m_new = jnp.maximum(m, jnp.max(s, axis=1, keepdims=True))
            alpha = jnp.exp(m - m_new)
            p = jnp.exp(s - m_new)
            l_new = alpha * l + jnp.sum(p, axis=1, keepdims=True)
            acc_ref[hh] = alpha * acc_ref[hh] + _dot(p.astype(BF16), vc)
            new += [m_new, l_new]
        return tuple(new)

    init = tuple(jnp.full((tq, 1), NEG, F32) if i % 2 == 0 else jnp.zeros((tq, 1), F32)
                 for i in range(4))
    n_diag = tq // tk
    carry = lax.fori_loop(0, qi * n_diag, step, init)
    for dd in range(n_diag):
        carry = step(qi * n_diag + dd, carry, diag_off=dd * tk)
    o0 = acc_ref[0] / carry[1]
    o1 = acc_ref[1] / carry[3]
    o_ref[...] = jnp.where(lane < HEAD_DIM, o0, o1).astype(o_ref.dtype)


def _prompt_attention(qb, kb, vb, fcum, tq=1024, tk=256):
    bsz, t, _ = qb.shape
    n_pairs = N_HEADS // 2
    nk = t // tk
    f_rows = fcum.transpose(0, 2, 1).reshape(bsz, n_pairs, 2, nk, tk).transpose(0, 1, 3, 2, 4)
    return pl.pallas_call(
        functools.partial(_attn_kernel, tq=tq, tk=tk),
        grid=(bsz, n_pairs, t // tq),
        in_specs=[pl.BlockSpec((None, tq, LANES), lambda b, p, i: (b, i, p)),
                  pl.BlockSpec((None, t, LANES), lambda b, p, i: (b, 0, p)),
                  pl.BlockSpec((None, t, LANES), lambda b, p, i: (b, 0, p)),
                  pl.BlockSpec((None, None, nk, 2, tk), lambda b, p, i: (b, p, 0, 0, 0))],
        out_specs=pl.BlockSpec((None, tq, LANES), lambda b, p, i: (b, i, p)),
        out_shape=jax.ShapeDtypeStruct((bsz, t, D_ATT), BF16),
        scratch_shapes=[pltpu.VMEM((2, tq, LANES), F32)],
        compiler_params=_cp(("parallel", "parallel", "arbitrary")),
        name="prompt_attn",
    )(qb, kb, vb, f_rows)


def _sattn_kernel(pt_ref, q_ref, *refs, n_new, n_pg):
    ck = refs[0:n_pg]
    cv = refs[n_pg:2 * n_pg]
    clf = refs[2 * n_pg:3 * n_pg]
    kn_ref, vn_ref, lfn_ref, triu_ref, o_ref, m_ref, l_ref, acc_ref, fcar_ref = refs[3 * n_pg:]
    p = pl.program_id(1)
    nq = n_new * N_HEADS
    lane = lax.broadcasted_iota(I32, (N_HEADS, D_ATT), 1)
    hrow = lax.broadcasted_iota(I32, (N_HEADS, D_ATT), 0)
    hmask = (lane // HEAD_DIM == hrow)

    @pl.when(p == 0)
    def _():
        m_ref[...] = jnp.full_like(m_ref, NEG)
        l_ref[...] = jnp.zeros_like(l_ref)
        acc_ref[...] = jnp.zeros_like(acc_ref)
        fcar_ref[...] = jnp.zeros_like(fcar_ref)

    q = q_ref[...]
    zq = jnp.zeros((N_HEADS, D_ATT), F32)
    qexp = jnp.concatenate(
        [jnp.where(hmask, jnp.broadcast_to(q[t:t + 1, :], (N_HEADS, D_ATT)), zq)
         for t in range(n_new)], axis=0).astype(BF16)

    def update(scores, values, transposed):
        m = m_ref[...]
        m_new = m
        for s in scores:
            m_new = jnp.maximum(m_new, jnp.max(s, axis=1, keepdims=True))
        alpha = jnp.exp(m - m_new)
        l_new = alpha * l_ref[...]
        acc = alpha * acc_ref[...]
        for s, vmat in zip(scores, values):
            pr = jnp.exp(s - m_new)
            l_new = l_new + jnp.sum(pr, axis=1, keepdims=True)
            pb = pr.astype(BF16)
            acc = acc + (_dot_nt(pb, vmat) if transposed else _dot(pb, vmat))
        l_ref[...] = l_new
        acc_ref[...] = acc
        m_ref[...] = m_new

    l1, l2, l3 = _split3(jnp.concatenate([r[...] for r in clf], axis=0))
    tri = triu_ref[...]
    local = _dot(l1, tri) + (_dot(l2, tri) + _dot(l3, tri))
    fcar = fcar_ref[...]
    scores = []
    for g in range(n_pg):
        fp = local[g * N_HEADS:(g + 1) * N_HEADS, :] + fcar
        fcar = jnp.broadcast_to(fp[:, PAGE_SIZE - 1:PAGE_SIZE], fcar.shape)
        scores.append(_dot(qexp, ck[g][...].astype(BF16))
                      - jnp.concatenate([fp] * n_new, axis=0))
    fcar_ref[...] = fcar
    update(scores, [cv[g][...].astype(BF16) for g in range(n_pg)], transposed=True)

    @pl.when(p == pl.num_programs(1) - 1)
    def _():
        lfn = lfn_ref[...]
        cols = []
        run = fcar[:, 0:1]
        for t in range(n_new):
            run = run + lfn[:, t:t + 1]
            cols.append(run)
        pad = SUBLANES - n_new
        fnew = jnp.concatenate(cols + [jnp.zeros((N_HEADS, pad), F32)], axis=1)
        zk = jnp.zeros((pad, D_ATT), F32)
        kn = jnp.concatenate([kn_ref[...], zk], axis=0).astype(BF16)
        vn = jnp.concatenate([vn_ref[...], zk], axis=0).astype(BF16)
        s2 = _dot_nt(qexp, kn) - jnp.concatenate([fnew] * n_new, axis=0)
        r = lax.broadcasted_iota(I32, (nq, SUBLANES), 0) // N_HEADS
        c = lax.broadcasted_iota(I32, (nq, SUBLANES), 1)
        update([jnp.where(c <= r, s2, NEG)], [vn], transposed=False)
        out = acc_ref[...] / l_ref[...]
        zo = jnp.zeros((N_HEADS, D_ATT), F32)
        rows = [jnp.sum(jnp.where(hmask, out[t * N_HEADS:(t + 1) * N_HEADS, :], zo),
                        axis=0, keepdims=True) for t in range(n_new)]
        o_ref[...] = jnp.concatenate(rows, axis=0)


def _sample_attention(qb, kn, vn, lfn, cache_k, cache_v, cache_logf, page_table, n_pg=8):
    n, s_new, _ = qb.shape
    n_pages = page_table.shape[1]
    n_phys = cache_k.shape[0]
    nq = s_new * N_HEADS
    ck = cache_k.transpose(0, 2, 3, 1).reshape(n_phys, D_ATT, PAGE_SIZE)
    cv = cache_v.transpose(0, 2, 3, 1).reshape(n_phys, D_ATT, PAGE_SIZE)
    clf = cache_logf.transpose(0, 2, 1)
    lfn_t = lfn.transpose(0, 2, 1)
    triu = jnp.triu(jnp.ones((PAGE_SIZE, PAGE_SIZE), BF16))
    seq = lambda i, p, pt: (i, 0, 0)
    tok_spec = pl.BlockSpec((None, s_new, D_ATT), seq)

    def page(g):
        return lambda i, p, pt: (pt[i * n_pages + p * n_pg + g], 0, 0)

    kv_specs = [pl.BlockSpec((None, D_ATT, PAGE_SIZE), page(g)) for g in range(n_pg)]
    lf_specs = [pl.BlockSpec((None, N_HEADS, PAGE_SIZE), page(g)) for g in range(n_pg)]
    gs = pltpu.PrefetchScalarGridSpec(
        num_scalar_prefetch=1,
        grid=(n, n_pages // n_pg),
        in_specs=[tok_spec] + kv_specs + kv_specs + lf_specs + [
            tok_spec, tok_spec,
            pl.BlockSpec((None, N_HEADS, s_new), seq),
            pl.BlockSpec(triu.shape, lambda i, p, pt: (0, 0))],
        out_specs=tok_spec,
        scratch_shapes=[pltpu.VMEM((nq, 1), F32), pltpu.VMEM((nq, 1), F32),
                        pltpu.VMEM((nq, D_ATT), F32),
                        pltpu.VMEM((N_HEADS, PAGE_SIZE), F32)],
    )
    return pl.pallas_call(
        functools.partial(_sattn_kernel, n_new=s_new, n_pg=n_pg),
        grid_spec=gs,
        out_shape=jax.ShapeDtypeStruct((n, s_new, D_ATT), F32),
        compiler_params=_cp(("parallel", "arbitrary")),
        name="sample_attn",
    )(page_table.reshape(-1), qb, *([ck] * n_pg), *([cv] * n_pg), *([clf] * n_pg),
      kn, vn, lfn_t, triu)


def _ln_swish(y, g, b):
    mu = jnp.mean(y, axis=-1, keepdims=True)
    yc = y - mu
    var = jnp.mean(yc * yc, axis=-1, keepdims=True)
    yn = yc * lax.rsqrt(var + EPS) * g + b
    return yn * jax.nn.sigmoid(yn)


def _conv_kernel(u_ref, cw_ref, cb_ref, g_ref, b_ref, o_ref, ext_ref, *, tm, hist):
    t = pl.program_id(1)

    @pl.when(t == 0)
    def _():
        ext_ref[0:hist, :] = jnp.zeros((hist, ext_ref.shape[1]), F32)

    ext_ref[hist:hist + tm, :] = u_ref[...]
    off = hist - (CONV_WIDTH - 1)
    y = jnp.zeros(u_ref.shape, F32) + cb_ref[...]
    for w in range(CONV_WIDTH):
        y = y + ext_ref[off + w:off + w + tm, :] * cw_ref[w:w + 1, :]
    o_ref[...] = _ln_swish(y, g_ref[...], b_ref[...]).astype(o_ref.dtype)
    ext_ref[0:hist, :] = ext_ref[tm:tm + hist, :]


def _conv_prompt(u, cw, cb, g, b, tm=512):
    bsz, t, c = u.shape
    hist = 32
    row = lambda bb, i: (bb, i, 0)
    const = lambda bb, i: (0, 0)
    return pl.pallas_call(
        functools.partial(_conv_kernel, tm=tm, hist=hist),
        grid=(bsz, t // tm),
        in_specs=[pl.BlockSpec((None, tm, c), row), pl.BlockSpec(cw.shape, const),
                  pl.BlockSpec((1, c), const), pl.BlockSpec((1, c), const),
                  pl.BlockSpec((1, c), const)],
        out_specs=pl.BlockSpec((None, tm, c), row),
        out_shape=jax.ShapeDtypeStruct((bsz, t, c), BF16),
        scratch_shapes=[pltpu.VMEM((hist + tm, c), F32)],
        compiler_params=_cp(("parallel", "arbitrary")),
        name="conv_prompt",
    )(u, cw, cb, g, b)


def _conv_step_kernel(st_ref, u_ref, cw_ref, cb_ref, g_ref, b_ref, o_ref, *, n_hist, n_new):
    def ext(i):
        return st_ref[i] if i < n_hist else u_ref[i - n_hist]

    for t in range(n_new):
        y = jnp.zeros(o_ref.shape[1:], F32) + cb_ref[...]
        for w in range(CONV_WIDTH):
            y = y + ext(t + w) * cw_ref[w:w + 1, :]
        o_ref[t] = _ln_swish(y, g_ref[...], b_ref[...]).astype(o_ref.dtype)


def _conv_sample(state_t, u_t, cw, cb, g, b):
    n_hist, n, c = state_t.shape
    n_new = u_t.shape[0]
    return pl.pallas_call(
        functools.partial(_conv_step_kernel, n_hist=n_hist, n_new=n_new),
        out_shape=jax.ShapeDtypeStruct((n_new, n, c), BF16),
        compiler_params=pltpu.CompilerParams(vmem_limit_bytes=VMEM_LIMIT),
        name="conv_sample",
    )(state_t, u_t, cw, cb, g, b)


def _tail_kernel(x_ref, a_ref, c_ref, g1_ref, sh_ref, sc_ref, n2_ref, woa_ref, woc_ref,
                 x1_ref, h2_ref):
    proj = (_dot(a_ref[...].astype(BF16), woa_ref[...])
            + _dot(c_ref[...].astype(BF16), woc_ref[...]))
    x1 = x_ref[...] + g1_ref[...] * proj
    x1_ref[...] = x1
    ms = jnp.mean(x1 * x1, axis=-1, keepdims=True)
    h2 = x1 * lax.rsqrt(ms + EPS) * n2_ref[...]
    h2_ref[...] = h2 * (1.0 + sc_ref[...]) + sh_ref[...]


def _tail(x, attn, conv, gate1, shift2, scale2, n2g, woa, woc, per_token, tm=256):
    bsz, t, d = x.shape
    row = lambda b, i: (b, i, 0)
    const2 = lambda b, i: (0, 0)
    mod_spec = (pl.BlockSpec((None, tm, d), row) if per_token
                else pl.BlockSpec((None, 1, d), lambda b, i: (b, 0, 0)))
    half = pl.BlockSpec((None, tm, D_ATT), row)
    full = pl.BlockSpec((None, tm, d), row)
    return pl.pallas_call(
        _tail_kernel,
        grid=(bsz, t // tm),
        in_specs=[full, half, half, mod_spec, mod_spec, mod_spec,
                  pl.BlockSpec(n2g.shape, const2), pl.BlockSpec(woa.shape, const2),
                  pl.BlockSpec(woc.shape, const2)],
        out_specs=[full, full],
        out_shape=[jax.ShapeDtypeStruct((bsz, t, d), F32)] * 2,
        compiler_params=_cp(("parallel", "parallel")),
        name="tail",
    )(x, attn, conv, gate1, shift2, scale2, n2g, woa, woc)


def _top_rows(s, order, payload, k):
    big = float(2 ** 24)
    vals, outs = [], []
    for _ in range(k):
        m = jnp.max(s, axis=0, keepdims=True)
        pos = jnp.min(jnp.where(s == m, order, big), axis=0, keepdims=True)
        hit = order == pos
        vals.append(m)
        if payload is None:
            outs.append(pos)
        else:
            outs.append(jnp.sum(jnp.where(hit, payload, 0), axis=0, keepdims=True))
        s = jnp.where(hit, -jnp.inf, s)
    return jnp.concatenate(vals, axis=0), jnp.concatenate(outs, axis=0)


def _pair_groups():
    groups = [(0, PEER_TOPK, PEER_TOPK)]
    for a in range(1, SUBLANES):
        groups.append((a, SUBLANES, PEER_TOPK // (a + 1)))
    groups.append((None, SUBLANES, SUBLANES))
    return groups


def _route_kernel(h_ref, wq_ref, skh_ref, skl_ref, e_ref, g_ref):
    hb = h_ref[...].astype(BF16)
    tb = hb.shape[0]
    key_order = lax.broadcasted_iota(I32, (PEER_N_KEYS, tb), 0).astype(F32)
    groups = _pair_groups()
    orders, masks = [], []
    for a, rows, valid in groups:
        r = lax.broadcasted_iota(I32, (rows, tb), 0)
        flat = (a * PEER_TOPK + r) if a is not None else (SUBLANES + r) * PEER_TOPK
        orders.append(flat.astype(F32))
        masks.append(r < valid)
    pair_order = jnp.concatenate(orders, axis=0)
    pair_valid = jnp.concatenate(masks, axis=0)
    e_rows, g_rows = [], []
    for hd in range(PEER_HEADS):
        tops = []
        for half in range(2):
            i = hd * 2 + half
            wq = wq_ref[i * PEER_HALF:(i + 1) * PEER_HALF, :]
            qt = _dot_nt(wq, hb)
            qh, ql = _split2(qt)
            skh = skh_ref[i]
            s = _dot(skh, qh) + (_dot(skh, ql) + _dot(skl_ref[i], qh))
            vals, pos = _top_rows(s, key_order, None, PEER_TOPK)
            tops.append((vals, pos.astype(I32)))
        (s1, i1), (s2, i2) = tops
        cand, cidx = [], []
        for a, rows, _ in groups:
            if a is not None:
                cand.append(s1[a:a + 1, :] + s2[0:rows, :])
                cidx.append(i1[a:a + 1, :] * PEER_N_KEYS + i2[0:rows, :])
            else:
                cand.append(s1[SUBLANES:, :] + s2[0:1, :])
                cidx.append(i1[SUBLANES:, :] * PEER_N_KEYS + i2[0:1, :])
        cand = jnp.where(pair_valid, jnp.concatenate(cand, axis=0), -jnp.inf)
        top, eidx = _top_rows(cand, pair_order, jnp.concatenate(cidx, axis=0), PEER_TOPK)
        ex = jnp.exp(top - top[0:1, :])
        g_rows.append(ex / jnp.sum(ex, axis=0, keepdims=True))
        e_rows.append(eidx)
    e_ref[...] = jnp.concatenate(e_rows, axis=0).T
    g_ref[...] = jnp.concatenate(g_rows, axis=0).T


def _route(h2, wq_t, skh, skl, tb=128):
    n, d = h2.shape
    const2 = lambda i: (0, 0)
    const3 = lambda i: (0, 0, 0)
    return pl.pallas_call(
        _route_kernel,
        grid=(n // tb,),
        in_specs=[pl.BlockSpec((tb, d), lambda i: (i, 0)), pl.BlockSpec(wq_t.shape, const2),
                  pl.BlockSpec(skh.shape, const3), pl.BlockSpec(skl.shape, const3)],
        out_specs=[pl.BlockSpec((tb, PEER_SLOTS), lambda i: (i, 0))] * 2,
        out_shape=[jax.ShapeDtypeStruct((n, PEER_SLOTS), I32),
                   jax.ShapeDtypeStruct((n, PEER_SLOTS), F32)],
        compiler_params=_cp(("parallel",)),
        name="peer_route",
    )(h2, wq_t, skh, skl)


def _peer_act_kernel(idx_ref, tab_ref, h_ref, g_ref, c_ref, ps0_ref, ps1_ref, ps2_ref, ps3_ref,
                     act_ref, *, tb):
    def gather(t, ps_ref):
        hv = h_ref[t]
        idx_t = idx_ref.at[pl.ds(t * PEER_SLOTS, PEER_SLOTS)]
        for j in range(PEER_SLOTS):
            ps_ref[j * SUBLANES:(j + 1) * SUBLANES, :] = tab_ref[idx_t[j]].astype(F32) * hv

    def reduce(t, ps_ref):
        s = ps_ref[pl.ds(0, PEER_SLOTS, stride=SUBLANES), :]
        for r in range(1, SUBLANES):
            s = s + ps_ref[pl.ds(r, PEER_SLOTS, stride=SUBLANES), :]
        act_ref[pl.ds(t, 1), :] = jnp.sum(s.T, axis=0, keepdims=True)

    ps = (ps0_ref, ps1_ref, ps2_ref, ps3_ref)

    per_trip = 4

    def trip(i, carry):
        for k in range(per_trip):
            gather(per_trip * i + k, ps[k % 4])
            reduce(jnp.maximum(per_trip * i + k - 2, 0), ps[(k + 2) % 4])
        return carry

    ps2_ref[...] = jnp.zeros_like(ps2_ref)
    ps3_ref[...] = jnp.zeros_like(ps3_ref)
    lax.fori_loop(0, tb // per_trip, trip, 0)
    reduce(tb - 2, ps2_ref)
    reduce(tb - 1, ps3_ref)
    act = act_ref[...]
    gelu = 0.5 * act * (1.0 + lax.erf(act * (2.0 ** -0.5)))
    c_ref[...] = g_ref[...] * gelu


def _peer_act(eidx, tab_u, h2, g, tb=64):
    n = eidx.shape[0]
    d = h2.shape[1]
    hc = h2.reshape(n, d // LANES, LANES)
    return pl.pallas_call(
        functools.partial(_peer_act_kernel, tb=tb),
        grid=(n // tb,),
        in_specs=[pl.BlockSpec((tb * PEER_SLOTS,), lambda i: (i,), memory_space=pltpu.SMEM),
                  pl.BlockSpec(memory_space=pltpu.VMEM),
                  pl.BlockSpec((tb, d // LANES, LANES), lambda i: (i, 0, 0)),
                  pl.BlockSpec((tb, PEER_SLOTS), lambda i: (i, 0))],
        out_specs=pl.BlockSpec((tb, PEER_SLOTS), lambda i: (i, 0)),
        out_shape=jax.ShapeDtypeStruct((n, PEER_SLOTS), F32),
        scratch_shapes=[pltpu.VMEM((PEER_SLOTS * SUBLANES, LANES), F32)] * 4
        + [pltpu.VMEM((tb, PEER_SLOTS), F32)],
        compiler_params=_cp(("arbitrary",)),
        name="peer_act",
    )(eidx.reshape(-1), tab_u, hc, g)


def _peer_out_kernel(idx_ref, tab_ref, c_ref, o_ref, cb_ref, *, tb):
    n_acc = 4

    def tok(t, carry):
        idx_t = idx_ref.at[pl.ds(t * PEER_SLOTS, PEER_SLOTS)]
        crow = c_ref[pl.ds(t, 1), :]
        cb_ref[...] = jnp.broadcast_to(crow, (PEER_SLOTS, PEER_SLOTS)).T
        accs = [jnp.zeros((SUBLANES, LANES), F32) for _ in range(n_acc)]
        for j in range(PEER_SLOTS):
            accs[j % n_acc] = (accs[j % n_acc]
                               + cb_ref[j:j + 1, :] * tab_ref[idx_t[j]].astype(F32))
        o_ref[t] = (accs[0] + accs[1]) + (accs[2] + accs[3])
        return carry

    lax.fori_loop(0, tb, tok, 0)


def _peer_out(eidx, coef, tab_v, tb=64):
    n = eidx.shape[0]
    chunks = tab_v.shape[1]
    out = pl.pallas_call(
        functools.partial(_peer_out_kernel, tb=tb),
        grid=(n // tb,),
        in_specs=[pl.BlockSpec((tb * PEER_SLOTS,), lambda i: (i,), memory_space=pltpu.SMEM),
                  pl.BlockSpec(memory_space=pltpu.VMEM),
                  pl.BlockSpec((tb, PEER_SLOTS), lambda i: (i, 0))],
        out_specs=pl.BlockSpec((tb, chunks, LANES), lambda i: (i, 0, 0)),
        out_shape=jax.ShapeDtypeStruct((n, chunks, LANES), F32),
        scratch_shapes=[pltpu.VMEM((PEER_SLOTS, PEER_SLOTS), F32)],
        compiler_params=_cp(("arbitrary",)),
        name="peer_out",
    )(eidx.reshape(-1), tab_v, coef)
    return out.reshape(n, chunks * LANES)


def _final_kernel(x1_ref, g2_ref, p_ref, y_ref):
    y_ref[...] = x1_ref[...] + g2_ref[...] * p_ref[...]


def _final(x1, gate2, peer, per_token, tm=512):
    bsz, t, d = x1.shape
    row = lambda b, i: (b, i, 0)
    mod_spec = (pl.BlockSpec((None, tm, d), row) if per_token
                else pl.BlockSpec((None, 1, d), lambda b, i: (b, 0, 0)))
    full = pl.BlockSpec((None, tm, d), row)
    return pl.pallas_call(
        _final_kernel,
        grid=(bsz, t // tm),
        in_specs=[full, mod_spec, full],
        out_specs=full,
        out_shape=jax.ShapeDtypeStruct((bsz, t, d), F32),
        compiler_params=_cp(("parallel", "parallel")),
        name="final",
    )(x1, gate2, peer)


def _prep_inproj(w_in, b_f, b_glu, q_norm_g, k_norm_g, tm):
    d = D_ATT
    nf = 3 * d + N_HEADS
    w = jnp.concatenate([w_in[:, :3 * d], w_in[:, nf:]], axis=1).astype(BF16)
    wf = jnp.pad(w_in[:, 3 * d:nf], ((0, 0), (0, LANES - N_HEADS)))
    wfh = wf.astype(BF16)
    wfl = (wf - wfh.astype(F32)).astype(BF16)
    bfp = jnp.pad(b_f, (0, LANES - N_HEADS)).reshape(1, LANES)
    qg = jnp.tile(q_norm_g, N_HEADS).reshape(1, d)
    kg = jnp.tile(k_norm_g, N_HEADS).reshape(1, d)
    hid = jnp.arange(d) // HEAD_DIM
    ones_bd = (hid[:, None] == hid[None, :]).astype(BF16)
    tri = jnp.tril(jnp.ones((tm, tm), BF16))
    return (w, wfh, wfl, bfp, b_glu.reshape(1, -1), qg, kg, ones_bd, tri)


def _peer_ffn(h2, wq_t, skh, skl, tab_u, tab_v):
    eidx, g = _route(h2, wq_t, skh, skl)
    coef = _peer_act(eidx, tab_u, h2, g)
    return _peer_out(eidx, coef, tab_v)


def kernel(x_prompt, x_sample, c_prompt, c_sample, cache_k, cache_v, cache_logf, state_conv,
           page_table, norm1_g, norm2_g, w_ada, b_ada, w_in, b_f, b_glu, q_norm_g, k_norm_g,
           conv_w, conv_b, conv_ln_g, conv_ln_b, w_o, peer_w_query, peer_sub_keys, peer_u,
           peer_v):
    depth = w_ada.shape[0]
    bsz, seq, d = x_prompt.shape
    n_dec, s_new, _ = x_sample.shape
    tm = 256
    xp, xs = x_prompt, x_sample.reshape(1, n_dec * s_new, d)
    outs = [[] for _ in range(8)]
    for l in range(depth):
        n_c = bsz + n_dec
        pad = (-n_c) % SUBLANES
        c_all = jnp.pad(jnp.concatenate([c_prompt, c_sample], axis=0), ((0, pad), (0, 0)))
        mod = _adaln(c_all, w_ada[l], b_ada[l])
        mod_p = [m.reshape(bsz, 1, d) for m in jnp.split(mod[:bsz], 6, axis=-1)]
        mod_s = [jnp.repeat(m, s_new, axis=0).reshape(1, n_dec * s_new, d)
                 for m in jnp.split(mod[bsz:n_c], 6, axis=-1)]

        wts = _prep_inproj(w_in[l], b_f[l], b_glu[l], q_norm_g[l], k_norm_g[l], tm)
        g1 = norm1_g[l].reshape(1, d)
        n2 = norm2_g[l].reshape(1, d)
        cw = conv_w[l]
        cb, lg, lb = (a[l].reshape(1, -1) for a in (conv_b, conv_ln_g, conv_ln_b))
        woa = w_o[l][:D_ATT].astype(BF16)
        woc = w_o[l][D_ATT:].astype(BF16)
        wq_t = peer_w_query[l].T.astype(BF16)
        sk = peer_sub_keys[l].reshape(PEER_HEADS * 2, PEER_N_KEYS, PEER_HALF)
        skh = sk.astype(BF16)
        skl = (sk - skh.astype(F32)).astype(BF16)
        tab_u = peer_u[l].astype(BF16).reshape(-1, d // LANES, LANES)
        tab_v = peer_v[l].astype(BF16).reshape(-1, d // LANES, LANES)

        sh1, sc1, gt1, sh2, sc2, gt2 = mod_p
        qb, k, v, kb, vb, lf, fc, u = _inproj(xp, sh1, sc1, g1, wts, per_token=False, tm=tm)
        a_out = _prompt_attention(qb, kb, vb, fc)
        c_out = _conv_prompt(u, cw, cb, lg, lb)
        x1, h2 = _tail(xp, a_out, c_out, gt1, sh2, sc2, n2, woa, woc, per_token=False, tm=tm)
        peer = _peer_ffn(h2.reshape(bsz * seq, d), wq_t, skh, skl, tab_u, tab_v)
        xp = _final(x1, gt2, peer.reshape(bsz, seq, d), per_token=False)
        outs[0].append(k.reshape(bsz, seq, N_HEADS, HEAD_DIM))
        outs[1].append(v.reshape(bsz, seq, N_HEADS, HEAD_DIM))
        outs[2].append(lf)
        outs[3].append(u[:, seq - (CONV_WIDTH - 1):, :])

        sh1, sc1, gt1, sh2, sc2, gt2 = mod_s
        qb, k, v, kb, vb, lf, fc, u = _inproj(xs, sh1, sc1, g1, wts, per_token=True, tm=tm)
        shp = (n_dec, s_new, D_ATT)
        a_out = _sample_attention(qb.reshape(shp).astype(F32), k.reshape(shp), v.reshape(shp),
                                  lf.reshape(n_dec, s_new, N_HEADS), cache_k[l], cache_v[l],
                                  cache_logf[l], page_table)
        u_s = u.reshape(shp)
        c_out = _conv_sample(state_conv[l].transpose(1, 0, 2), u_s.transpose(1, 0, 2),
                             cw, cb, lg, lb).transpose(1, 0, 2)
        x1, h2 = _tail(xs, a_out.reshape(1, -1, D_ATT), c_out.reshape(1, -1, D_ATT), gt1, sh2,
                       sc2, n2, woa, woc, per_token=True, tm=tm)
        peer = _peer_ffn(h2.reshape(n_dec * s_new, d), wq_t, skh, skl, tab_u, tab_v)
        xs = _final(x1, gt2, peer.reshape(1, n_dec * s_new, d), per_token=True)
        outs[4].append(k.reshape(n_dec, s_new, N_HEADS, HEAD_DIM))
        outs[5].append(v.reshape(n_dec, s_new, N_HEADS, HEAD_DIM))
        outs[6].append(lf.reshape(n_dec, s_new, N_HEADS))
        outs[7].append(jnp.concatenate([state_conv[l], u_s], axis=1)[:, -(CONV_WIDTH - 1):])
    st = [jnp.stack(o) for o in outs]
    return (xp, xs.reshape(n_dec, s_new, d), st[0], st[1], st[2], st[3], st[4], st[5], st[6],
            st[7])
```

```python
import functools

import jax
import jax.numpy as jnp
from jax import lax
from jax.experimental import pallas as pl
from jax.experimental.pallas import tpu as pltpu

F32 = jnp.float32
BF16 = jnp.bfloat16
I32 = jnp.int32

EPS = 1e-6
N_HEADS = 8
HEAD_DIM = 64
D_ATT = N_HEADS * HEAD_DIM
CONV_WIDTH = 31
PAGE_SIZE = 128
PEER_HEADS = 8
PEER_N_KEYS = 128
PEER_HALF = 128
PEER_TOPK = 16
PEER_SLOTS = PEER_HEADS * PEER_TOPK
ROW_UNITS = 4
LANES = 128
SUBLANES = 8
NEG = -1e30
VMEM_LIMIT = 56 * 1024 * 1024


def _cp(sem, vmem=VMEM_LIMIT):
    return pltpu.CompilerParams(dimension_semantics=sem, vmem_limit_bytes=vmem)


def _dot(a, b):
    return jnp.dot(a, b, preferred_element_type=F32)


def _dot_nt(a, b):
    return lax.dot_general(a, b, (((1,), (1,)), ((), ())), preferred_element_type=F32)


def _split2(a):
    hi = a.astype(BF16)
    lo = (a - hi.astype(F32)).astype(BF16)
    return hi, lo


def _split3(a):
    hi = a.astype(BF16)
    r = a - hi.astype(F32)
    mid = r.astype(BF16)
    lo = (r - mid.astype(F32)).astype(BF16)
    return hi, mid, lo


def _log_sigmoid(x):
    return jnp.minimum(x, 0.0) - jnp.log1p(jnp.exp(-jnp.abs(x)))


def _ada_kernel(c_ref, w_ref, b_ref, o_ref):
    ch, cl = _split2(c_ref[...])
    wh, wl = _split2(w_ref[...])
    o_ref[...] = _dot(ch, wh) + (_dot(ch, wl) + _dot(cl, wh)) + b_ref[...]


def _adaln(c, w_ada, b_ada):
    r, d = c.shape
    n = w_ada.shape[1]
    tn = 512
    return pl.pallas_call(
        _ada_kernel,
        grid=(n // tn,),
        in_specs=[pl.BlockSpec((r, d), lambda j: (0, 0)),
                  pl.BlockSpec((d, tn), lambda j: (0, j)),
                  pl.BlockSpec((1, tn), lambda j: (0, j))],
        out_specs=pl.BlockSpec((r, tn), lambda j: (0, j)),
        out_shape=jax.ShapeDtypeStruct((r, n), F32),
        compiler_params=_cp(("parallel",)),
        name="adaln",
    )(c, w_ada, b_ada.reshape(1, n))


def _inproj_kernel(x_ref, sh_ref, sc_ref, g1_ref, w_ref, wfh_ref, wfl_ref, bf_ref, bglu_ref,
                   qg_ref, kg_ref, ones_ref, tri_ref,
                   q_ref, k_ref, v_ref, kb_ref, vb_ref, lf_ref, fc_ref, u_ref, carry_ref):
    t = pl.program_id(1)

    @pl.when(t == 0)
    def _():
        carry_ref[...] = jnp.zeros_like(carry_ref)

    x = x_ref[...]
    ms = jnp.mean(x * x, axis=-1, keepdims=True)
    h = x * lax.rsqrt(ms + EPS) * g1_ref[...]
    h = h * (1.0 + sc_ref[...]) + sh_ref[...]
    hh, hl = _split2(h)
    z = _dot(hh, w_ref[...])
    fg = _dot(hh, wfh_ref[...]) + (_dot(hh, wfl_ref[...]) + _dot(hl, wfh_ref[...]))
    d = D_ATT
    q = z[:, 0:d]
    k = z[:, d:2 * d]
    v = z[:, 2 * d:3 * d]
    a = z[:, 3 * d:4 * d] + bglu_ref[:, 0:d]
    b = z[:, 4 * d:5 * d] + bglu_ref[:, d:2 * d]
    qs = _dot((q * q).astype(BF16), ones_ref[...])
    ks = _dot((k * k).astype(BF16), ones_ref[...])
    qn = q * lax.rsqrt(qs * (1.0 / HEAD_DIM) + EPS) * qg_ref[...]
    kn = k * lax.rsqrt(ks * (1.0 / HEAD_DIM) + EPS) * kg_ref[...]
    q_ref[...] = (qn * (HEAD_DIM ** -0.5)).astype(BF16)
    k_ref[...] = kn
    kb_ref[...] = kn.astype(BF16)
    v_ref[...] = v
    vb_ref[...] = v.astype(BF16)
    u_ref[...] = a * jax.nn.sigmoid(b)
    lf = _log_sigmoid(fg + bf_ref[...])
    lf_ref[...] = lf[:, 0:N_HEADS]
    l1, l2, l3 = _split3(lf)
    tri = tri_ref[...]
    fc = _dot(tri, l1) + (_dot(tri, l2) + _dot(tri, l3)) + carry_ref[...]
    fc_ref[...] = fc[:, 0:N_HEADS]
    tm = fc.shape[0]
    carry_ref[...] = fc[tm - 1:tm, :]


def _inproj(x, shift, scale, g1, wts, per_token, tm=256):
    bsz, t, d = x.shape
    w, wfh, wfl, bfp, bglu, qg, kg, ones_bd, tri = wts
    row = lambda b, i: (b, i, 0)
    const2 = lambda b, i: (0, 0)
    mod_spec = (pl.BlockSpec((None, tm, d), row) if per_token
                else pl.BlockSpec((None, 1, d), lambda b, i: (b, 0, 0)))
    full = lambda arr: pl.BlockSpec(arr.shape, const2)
    outs = [
        jax.ShapeDtypeStruct((bsz, t, D_ATT), BF16),
        jax.ShapeDtypeStruct((bsz, t, D_ATT), F32),
        jax.ShapeDtypeStruct((bsz, t, D_ATT), F32),
        jax.ShapeDtypeStruct((bsz, t, D_ATT), BF16),
        jax.ShapeDtypeStruct((bsz, t, D_ATT), BF16),
        jax.ShapeDtypeStruct((bsz, t, N_HEADS), F32),
        jax.ShapeDtypeStruct((bsz, t, N_HEADS), F32),
        jax.ShapeDtypeStruct((bsz, t, D_ATT), F32),
    ]
    ospec = lambda s: pl.BlockSpec((None, tm, s.shape[2]), row)
    return pl.pallas_call(
        _inproj_kernel,
        grid=(bsz, t // tm),
        in_specs=[pl.BlockSpec((None, tm, d), row), mod_spec, mod_spec, full(g1), full(w),
                  full(wfh), full(wfl), full(bfp), full(bglu), full(qg), full(kg),
                  full(ones_bd), full(tri)],
        out_specs=[ospec(s) for s in outs],
        out_shape=outs,
        scratch_shapes=[pltpu.VMEM((1, LANES), F32)],
        compiler_params=_cp(("parallel", "arbitrary")),
        name="inproj",
    )(x, shift, scale, g1, w, wfh, wfl, bfp, bglu, qg, kg, ones_bd, tri)


def _attn_kernel(q_ref, k_ref, v_ref, f_ref, o_ref, acc_ref, *, tq, tk):
    qi = pl.program_id(2)
    q = q_ref[...]
    lane = lax.broadcasted_iota(I32, (1, LANES), 1)
    zero = jnp.zeros_like(q)
    qm = [jnp.where(lane < HEAD_DIM, q, zero), jnp.where(lane >= HEAD_DIM, q, zero)]
    acc_ref[...] = jnp.zeros_like(acc_ref)
    rows = lax.broadcasted_iota(I32, (tq, tk), 0)
    cols = lax.broadcasted_iota(I32, (tq, tk), 1)

    def step(j, carry, diag_off=None):
        start = pl.multiple_of(j * tk, tk)
        kc = k_ref[pl.ds(start, tk), :]
        vc = v_ref[pl.ds(start, tk), :]
        fr = f_ref[j]
        new = []
        for hh in range(2):
            m, l = carry[2 * hh], carry[2 * hh + 1]
            s = _dot_nt(qm[hh], kc) - fr[hh:hh + 1, :]
            if diag_off is not None:
                s = jnp.where(cols + diag_off <= rows, s, NEG)
            m_new = jnp.maximum(m, jnp.max(s, axis=1, keepdims=True))
            alpha = jnp.exp(m - m_new)
            p = jnp.exp(s - m_new)
            l_new = alpha * l + jnp.sum(p, axis=1, keepdims=True)
            acc_ref[hh] = alpha * acc_ref[hh] + _dot(p.astype(BF16), vc)
            new += [m_new, l_new]
        return tuple(new)

    init = tuple(jnp.full((tq, 1), NEG, F32) if i % 2 == 0 else jnp.zeros((tq, 1), F32)
                 for i in range(4))
    n_diag = tq // tk
    carry = lax.fori_loop(0, qi * n_diag, step, init)
    for dd in range(n_diag):
        carry = step(qi * n_diag + dd, carry, diag_off=dd * tk)
    o0 = acc_ref[0] / carry[1]
    o1 = acc_ref[1] / carry[3]
    o_ref[...] = jnp.where(lane < HEAD_DIM, o0, o1).astype(o_ref.dtype)


def _prompt_attention(qb, kb, vb, fcum, tq=1024, tk=256):
    bsz, t, _ = qb.shape
    n_pairs = N_HEADS // 2
    nk = t // tk
    f_rows = fcum.transpose(0, 2, 1).reshape(bsz, n_pairs, 2, nk, tk).transpose(0, 1, 3, 2, 4)
    return pl.pallas_call(
        functools.partial(_attn_kernel, tq=tq, tk=tk),
        grid=(bsz, n_pairs, t // tq),
        in_specs=[pl.BlockSpec((None, tq, LANES), lambda b, p, i: (b, i, p)),
                  pl.BlockSpec((None, t, LANES), lambda b, p, i: (b, 0, p)),
                  pl.BlockSpec((None, t, LANES), lambda b, p, i: (b, 0, p)),
                  pl.BlockSpec((None, None, nk, 2, tk), lambda b, p, i: (b, p, 0, 0, 0))],
        out_specs=pl.BlockSpec((None, tq, LANES), lambda b, p, i: (b, i, p)),
        out_shape=jax.ShapeDtypeStruct((bsz, t, D_ATT), BF16),
        scratch_shapes=[pltpu.VMEM((2, tq, LANES), F32)],
        compiler_params=_cp(("parallel", "parallel", "arbitrary")),
        name="prompt_attn",
    )(qb, kb, vb, f_rows)


def _sattn_kernel(pt_ref, q_ref, *refs, n_new, n_pg):
    ck = refs[0:n_pg]
    cv = refs[n_pg:2 * n_pg]
    clf = refs[2 * n_pg:3 * n_pg]
    kn_ref, vn_ref, lfn_ref, triu_ref, o_ref, m_ref, l_ref, acc_ref, fcar_ref = refs[3 * n_pg:]
    p = pl.program_id(1)
    nq = n_new * N_HEADS
    lane = lax.broadcasted_iota(I32, (N_HEADS, D_ATT), 1)
    hrow = lax.broadcasted_iota(I32, (N_HEADS, D_ATT), 0)
    hmask = (lane // HEAD_DIM == hrow)

    @pl.when(p == 0)
    def _():
        m_ref[...] = jnp.full_like(m_ref, NEG)
        l_ref[...] = jnp.zeros_like(l_ref)
        acc_ref[...] = jnp.zeros_like(acc_ref)
        fcar_ref[...] = jnp.zeros_like(fcar_ref)

    q = q_ref[...]
    zq = jnp.zeros((N_HEADS, D_ATT), F32)
    qexp = jnp.concatenate(
        [jnp.where(hmask, jnp.broadcast_to(q[t:t + 1, :], (N_HEADS, D_ATT)), zq)
         for t in range(n_new)], axis=0).astype(BF16)

    def update(scores, values, transposed):
        m = m_ref[...]
        m_new = m
        for s in scores:
            m_new = jnp.maximum(m_new, jnp.max(s, axis=1, keepdims=True))
        alpha = jnp.exp(m - m_new)
        l_new = alpha * l_ref[...]
        acc = alpha * acc_ref[...]
        for s, vmat in zip(scores, values):
            pr = jnp.exp(s - m_new)
            l_new = l_new + jnp.sum(pr, axis=1, keepdims=True)
            pb = pr.astype(BF16)
            acc = acc + (_dot_nt(pb, vmat) if transposed else _dot(pb, vmat))
        l_ref[...] = l_new
        acc_ref[...] = acc
        m_ref[...] = m_new

    l1, l2, l3 = _split3(jnp.concatenate([r[...] for r in clf], axis=0))
    tri = triu_ref[...]
    local = _dot(l1, tri) + (_dot(l2, tri) + _dot(l3, tri))
    fcar = fcar_ref[...]
    scores = []
    for g in range(n_pg):
        fp = local[g * N_HEADS:(g + 1) * N_HEADS, :] + fcar
        fcar = jnp.broadcast_to(fp[:, PAGE_SIZE - 1:PAGE_SIZE], fcar.shape)
        scores.append(_dot(qexp, ck[g][...].astype(BF16))
                      - jnp.concatenate([fp] * n_new, axis=0))
    fcar_ref[...] = fcar
    update(scores, [cv[g][...].astype(BF16) for g in range(n_pg)], transposed=True)

    @pl.when(p == pl.num_programs(1) - 1)
    def _():
        lfn = lfn_ref[...]
        cols = []
        run = fcar[:, 0:1]
        for t in range(n_new):
            run = run + lfn[:, t:t + 1]
            cols.append(run)
        pad = SUBLANES - n_new
        fnew = jnp.concatenate(cols + [jnp.zeros((N_HEADS, pad), F32)], axis=1)
        zk = jnp.zeros((pad, D_ATT), F32)
        kn = jnp.concatenate([kn_ref[...], zk], axis=0).astype(BF16)
        vn = jnp.concatenate([vn_ref[...], zk], axis=0).astype(BF16)
        s2 = _dot_nt(qexp, kn) - jnp.concatenate([fnew] * n_new, axis=0)
        r = lax.broadcasted_iota(I32, (nq, SUBLANES), 0) // N_HEADS
        c = lax.broadcasted_iota(I32, (nq, SUBLANES), 1)
        update([jnp.where(c <= r, s2, NEG)], [vn], transposed=False)
        out = acc_ref[...] / l_ref[...]
        zo = jnp.zeros((N_HEADS, D_ATT), F32)
        rows = [jnp.sum(jnp.where(hmask, out[t * N_HEADS:(t + 1) * N_HEADS, :], zo),
                        axis=0, keepdims=True) for t in range(n_new)]
        o_ref[...] = jnp.concatenate(rows, axis=0)


def _sample_attention(qb, kn, vn, lfn, cache_k, cache_v, cache_logf, page_table, n_pg=8):
    n, s_new, _ = qb.shape
    n_pages = page_table.shape[1]
    n_phys = cache_k.shape[0]
    nq = s_new * N_HEADS
    ck = cache_k.transpose(0, 2, 3, 1).reshape(n_phys, D_ATT, PAGE_SIZE)
    cv = cache_v.transpose(0, 2, 3, 1).reshape(n_phys, D_ATT, PAGE_SIZE)
    clf = cache_logf.transpose(0, 2, 1)
    lfn_t = lfn.transpose(0, 2, 1)
    triu = jnp.triu(jnp.ones((PAGE_SIZE, PAGE_SIZE), BF16))
    seq = lambda i, p, pt: (i, 0, 0)
    tok_spec = pl.BlockSpec((None, s_new, D_ATT), seq)

    def page(g):
        return lambda i, p, pt: (pt[i * n_pages + p * n_pg + g], 0, 0)

    kv_specs = [pl.BlockSpec((None, D_ATT, PAGE_SIZE), page(g)) for g in range(n_pg)]
    lf_specs = [pl.BlockSpec((None, N_HEADS, PAGE_SIZE), page(g)) for g in range(n_pg)]
    gs = pltpu.PrefetchScalarGridSpec(
        num_scalar_prefetch=1,
        grid=(n, n_pages // n_pg),
        in_specs=[tok_spec] + kv_specs + kv_specs + lf_specs + [
            tok_spec, tok_spec,
            pl.BlockSpec((None, N_HEADS, s_new), seq),
            pl.BlockSpec(triu.shape, lambda i, p, pt: (0, 0))],
        out_specs=tok_spec,
        scratch_shapes=[pltpu.VMEM((nq, 1), F32), pltpu.VMEM((nq, 1), F32),
                        pltpu.VMEM((nq, D_ATT), F32),
                        pltpu.VMEM((N_HEADS, PAGE_SIZE), F32)],
    )
    return pl.pallas_call(
        functools.partial(_sattn_kernel, n_new=s_new, n_pg=n_pg),
        grid_spec=gs,
        out_shape=jax.ShapeDtypeStruct((n, s_new, D_ATT), F32),
        compiler_params=_cp(("parallel", "arbitrary")),
        name="sample_attn",
    )(page_table.reshape(-1), qb, *([ck] * n_pg), *([cv] * n_pg), *([clf] * n_pg),
      kn, vn, lfn_t, triu)


def _ln_swish(y, g, b):
    mu = jnp.mean(y, axis=-1, keepdims=True)
    yc = y - mu
    var = jnp.mean(yc * yc, axis=-1, keepdims=True)
    yn = yc * lax.rsqrt(var + EPS) * g + b
    return yn * jax.nn.sigmoid(yn)


def _conv_kernel(u_ref, cw_ref, cb_ref, g_ref, b_ref, o_ref, ext_ref, *, tm, hist):
    t = pl.program_id(1)

    @pl.when(t == 0)
    def _():
        ext_ref[0:hist, :] = jnp.zeros((hist, ext_ref.shape[1]), F32)

    ext_ref[hist:hist + tm, :] = u_ref[...]
    off = hist - (CONV_WIDTH - 1)
    y = jnp.zeros(u_ref.shape, F32) + cb_ref[...]
    for w in range(CONV_WIDTH):
        y = y + ext_ref[off + w:off + w + tm, :] * cw_ref[w:w + 1, :]
    o_ref[...] = _ln_swish(y, g_ref[...], b_ref[...]).astype(o_ref.dtype)
    ext_ref[0:hist, :] = ext_ref[tm:tm + hist, :]


def _conv_prompt(u, cw, cb, g, b, tm=512):
    bsz, t, c = u.shape
    hist = 32
    row = lambda bb, i: (bb, i, 0)
    const = lambda bb, i: (0, 0)
    return pl.pallas_call(
        functools.partial(_conv_kernel, tm=tm, hist=hist),
        grid=(bsz, t // tm),
        in_specs=[pl.BlockSpec((None, tm, c), row), pl.BlockSpec(cw.shape, const),
                  pl.BlockSpec((1, c), const), pl.BlockSpec((1, c), const),
                  pl.BlockSpec((1, c), const)],
        out_specs=pl.BlockSpec((None, tm, c), row),
        out_shape=jax.ShapeDtypeStruct((bsz, t, c), BF16),
        scratch_shapes=[pltpu.VMEM((hist + tm, c), F32)],
        compiler_params=_cp(("parallel", "arbitrary")),
        name="conv_prompt",
    )(u, cw, cb, g, b)


def _conv_step_kernel(st_ref, u_ref, cw_ref, cb_ref, g_ref, b_ref, o_ref, *, n_hist, n_new):
    def ext(i):
        return st_ref[i] if i < n_hist else u_ref[i - n_hist]

    for t in range(n_new):
        y = jnp.zeros(o_ref.shape[1:], F32) + cb_ref[...]
        for w in range(CONV_WIDTH):
            y = y + ext(t + w) * cw_ref[w:w + 1, :]
        o_ref[t] = _ln_swish(y, g_ref[...], b_ref[...]).astype(o_ref.dtype)


def _conv_sample(state_t, u_t, cw, cb, g, b):
    n_hist, n, c = state_t.shape
    n_new = u_t.shape[0]
    return pl.pallas_call(
        functools.partial(_conv_step_kernel, n_hist=n_hist, n_new=n_new),
        out_shape=jax.ShapeDtypeStruct((n_new, n, c), BF16),
        compiler_params=pltpu.CompilerParams(vmem_limit_bytes=VMEM_LIMIT),
        name="conv_sample",
    )(state_t, u_t, cw, cb, g, b)


def _tail_kernel(x_ref, a_ref, c_ref, g1_ref, sh_ref, sc_ref, n2_ref, woa_ref, woc_ref,
                 x1_ref, h2_ref):
    proj = (_dot(a_ref[...].astype(BF16), woa_ref[...])
            + _dot(c_ref[...].astype(BF16), woc_ref[...]))
    x1 = x_ref[...] + g1_ref[...] * proj
    x1_ref[...] = x1
    ms = jnp.mean(x1 * x1, axis=-1, keepdims=True)
    h2 = x1 * lax.rsqrt(ms + EPS) * n2_ref[...]
    h2_ref[...] = h2 * (1.0 + sc_ref[...]) + sh_ref[...]


def _tail(x, attn, conv, gate1, shift2, scale2, n2g, woa, woc, per_token, tm=256):
    bsz, t, d = x.shape
    row = lambda b, i: (b, i, 0)
    const2 = lambda b, i: (0, 0)
    mod_spec = (pl.BlockSpec((None, tm, d), row) if per_token
                else pl.BlockSpec((None, 1, d), lambda b, i: (b, 0, 0)))
    half = pl.BlockSpec((None, tm, D_ATT), row)
    full = pl.BlockSpec((None, tm, d), row)
    return pl.pallas_call(
        _tail_kernel,
        grid=(bsz, t // tm),
        in_specs=[full, half, half, mod_spec, mod_spec, mod_spec,
                  pl.BlockSpec(n2g.shape, const2), pl.BlockSpec(woa.shape, const2),
                  pl.BlockSpec(woc.shape, const2)],
        out_specs=[full, full],
        out_shape=[jax.ShapeDtypeStruct((bsz, t, d), F32)] * 2,
        compiler_params=_cp(("parallel", "parallel")),
        name="tail",
    )(x, attn, conv, gate1, shift2, scale2, n2g, woa, woc)


def _top_rows(s, order, payload, k):
    big = float(2 ** 24)
    vals, outs = [], []
    for _ in range(k):
        m = jnp.max(s, axis=0, keepdims=True)
        pos = jnp.min(jnp.where(s == m, order, big), axis=0, keepdims=True)
        hit = order == pos
        vals.append(m)
        if payload is None:
            outs.append(pos)
        else:
            outs.append(jnp.sum(jnp.where(hit, payload, 0), axis=0, keepdims=True))
        s = jnp.where(hit, -jnp.inf, s)
    return jnp.concatenate(vals, axis=0), jnp.concatenate(outs, axis=0)


def _pair_groups():
    groups = [(0, PEER_TOPK, PEER_TOPK)]
    for a in range(1, SUBLANES):
        groups.append((a, SUBLANES, PEER_TOPK // (a + 1)))
    groups.append((None, SUBLANES, SUBLANES))
    return groups


def _route_kernel(h_ref, wq_ref, skh_ref, skl_ref, e_ref, g_ref):
    hb = h_ref[...].astype(BF16)
    tb = hb.shape[0]
    key_order = lax.broadcasted_iota(I32, (PEER_N_KEYS, tb), 0).astype(F32)
    groups = _pair_groups()
    orders, masks = [], []
    for a, rows, valid in groups:
        r = lax.broadcasted_iota(I32, (rows, tb), 0)
        flat = (a * PEER_TOPK + r) if a is not None else (SUBLANES + r) * PEER_TOPK
        orders.append(flat.astype(F32))
        masks.append(r < valid)
    pair_order = jnp.concatenate(orders, axis=0)
    pair_valid = jnp.concatenate(masks, axis=0)
    e_rows, g_rows = [], []
    for hd in range(PEER_HEADS):
        tops = []
        for half in range(2):
            i = hd * 2 + half
            wq = wq_ref[i * PEER_HALF:(i + 1) * PEER_HALF, :]
            qt = _dot_nt(wq, hb)
            qh, ql = _split2(qt)
            skh = skh_ref[i]
            s = _dot(skh, qh) + (_dot(skh, ql) + _dot(skl_ref[i], qh))
            vals, pos = _top_rows(s, key_order, None, PEER_TOPK)
            tops.append((vals, pos.astype(I32)))
        (s1, i1), (s2, i2) = tops
        cand, cidx = [], []
        for a, rows, _ in groups:
            if a is not None:
                cand.append(s1[a:a + 1, :] + s2[0:rows, :])
                cidx.append(i1[a:a + 1, :] * PEER_N_KEYS + i2[0:rows, :])
            else:
                cand.append(s1[SUBLANES:, :] + s2[0:1, :])
                cidx.append(i1[SUBLANES:, :] * PEER_N_KEYS + i2[0:1, :])
        cand = jnp.where(pair_valid, jnp.concatenate(cand, axis=0), -jnp.inf)
        top, eidx = _top_rows(cand, pair_order, jnp.concatenate(cidx, axis=0), PEER_TOPK)
        ex = jnp.exp(top - top[0:1, :])
        g_rows.append(ex / jnp.sum(ex, axis=0, keepdims=True))
        e_rows.append(eidx)
    e_ref[...] = jnp.concatenate(e_rows, axis=0).T
    g_ref[...] = jnp.concatenate(g_rows, axis=0).T


def _route(h2, wq_t, skh, skl, tb=128):
    n, d = h2.shape
    const2 = lambda i: (0, 0)
    const3 = lambda i: (0, 0, 0)
    return pl.pallas_call(
        _route_kernel,
        grid=(n // tb,),
        in_specs=[pl.BlockSpec((tb, d), lambda i: (i, 0)), pl.BlockSpec(wq_t.shape, const2),
                  pl.BlockSpec(skh.shape, const3), pl.BlockSpec(skl.shape, const3)],
        out_specs=[pl.BlockSpec((tb, PEER_SLOTS), lambda i: (i, 0))] * 2,
        out_shape=[jax.ShapeDtypeStruct((n, PEER_SLOTS), I32),
                   jax.ShapeDtypeStruct((n, PEER_SLOTS), F32)],
        compiler_params=_cp(("parallel",)),
        name="peer_route",
    )(h2, wq_t, skh, skl)


def _pack_table(tab):
    e, d = tab.shape
    bits = lax.bitcast_convert_type(tab.astype(BF16), jnp.uint16).astype(jnp.uint32)
    words = bits[:, :d // 2] | (bits[:, d // 2:] << 16)
    return jnp.pad(words.reshape(e * ROW_UNITS, LANES), ((ROW_UNITS, ROW_UNITS), (0, 0)))


def _row_offsets(eidx):
    slot = jnp.arange(eidx.shape[1])
    return (eidx * ROW_UNITS + jnp.where(slot % 2 == 0, ROW_UNITS, 0)).reshape(-1)


def _load_pair(tab_ref, off_even, off_odd):
    low = lax.broadcasted_iota(I32, (SUBLANES, LANES), 0) < ROW_UNITS
    w = jnp.where(low, tab_ref[pl.ds(off_even, SUBLANES), :], tab_ref[pl.ds(off_odd, SUBLANES), :])
    first = lax.bitcast_convert_type(w << 16, F32)
    second = lax.bitcast_convert_type(w & jnp.uint32(0xFFFF0000), F32)
    return first, second


def _peer_act_kernel(idx_ref, tab_ref, h_ref, g_ref, c_ref, ps0_ref, ps1_ref, ps2_ref, ps3_ref,
                     act_ref, *, tb):
    def gather(t, ps_ref):
        hv = h_ref[t]
        h_first = jnp.concatenate([hv[0:ROW_UNITS, :]] * 2, axis=0)
        h_second = jnp.concatenate([hv[ROW_UNITS:, :]] * 2, axis=0)
        idx_t = idx_ref.at[pl.ds(t * PEER_SLOTS, PEER_SLOTS)]
        for p in range(PEER_SLOTS // 2):
            first, second = _load_pair(tab_ref, idx_t[2 * p], idx_t[2 * p + 1])
            ps_ref[p * SUBLANES:(p + 1) * SUBLANES, :] = first * h_first + second * h_second

    def reduce(t, ps_ref):
        s = ps_ref[pl.ds(0, PEER_SLOTS, stride=ROW_UNITS), :]
        for r in range(1, ROW_UNITS):
            s = s + ps_ref[pl.ds(r, PEER_SLOTS, stride=ROW_UNITS), :]
        act_ref[pl.ds(t, 1), :] = jnp.sum(s.T, axis=0, keepdims=True)

    ps = (ps0_ref, ps1_ref, ps2_ref, ps3_ref)

    per_trip = 4

    def trip(i, carry):
        for k in range(per_trip):
            gather(per_trip * i + k, ps[k % 4])
            reduce(jnp.maximum(per_trip * i + k - 2, 0), ps[(k + 2) % 4])
        return carry

    ps2_ref[...] = jnp.zeros_like(ps2_ref)
    ps3_ref[...] = jnp.zeros_like(ps3_ref)
    lax.fori_loop(0, tb // per_trip, trip, 0)
    reduce(tb - 2, ps2_ref)
    reduce(tb - 1, ps3_ref)
    act = act_ref[...]
    gelu = 0.5 * act * (1.0 + lax.erf(act * (2.0 ** -0.5)))
    c_ref[...] = g_ref[...] * gelu


def _peer_act(eidx, tab_u, h2, g, tb=64):
    n = eidx.shape[0]
    d = h2.shape[1]
    hc = h2.reshape(n, d // LANES, LANES)
    return pl.pallas_call(
        functools.partial(_peer_act_kernel, tb=tb),
        grid=(n // tb,),
        in_specs=[pl.BlockSpec((tb * PEER_SLOTS,), lambda i: (i,), memory_space=pltpu.SMEM),
                  pl.BlockSpec(memory_space=pltpu.VMEM),
                  pl.BlockSpec((tb, d // LANES, LANES), lambda i: (i, 0, 0)),
                  pl.BlockSpec((tb, PEER_SLOTS), lambda i: (i, 0))],
        out_specs=pl.BlockSpec((tb, PEER_SLOTS), lambda i: (i, 0)),
        out_shape=jax.ShapeDtypeStruct((n, PEER_SLOTS), F32),
        scratch_shapes=[pltpu.VMEM((PEER_SLOTS * ROW_UNITS, LANES), F32)] * 4
        + [pltpu.VMEM((tb, PEER_SLOTS), F32)],
        compiler_params=_cp(("arbitrary",)),
        name="peer_act",
    )(_row_offsets(eidx), tab_u, hc, g)


def _peer_out_kernel(idx_ref, tab_ref, c_ref, o_ref, cb0_ref, cb1_ref, *, tb):
    n_acc = 4

    def prepare(t, cb_ref):
        crow = c_ref[pl.ds(t, 1), :]
        cb_ref[...] = jnp.broadcast_to(crow, (PEER_SLOTS, PEER_SLOTS)).T

    def tok(t, cb_ref):
        idx_t = idx_ref.at[pl.ds(t * PEER_SLOTS, PEER_SLOTS)]
        low = lax.broadcasted_iota(I32, (SUBLANES, LANES), 0) < ROW_UNITS
        lo = [jnp.zeros((SUBLANES, LANES), F32) for _ in range(n_acc)]
        hi = [jnp.zeros((SUBLANES, LANES), F32) for _ in range(n_acc)]
        for p in range(PEER_SLOTS // 2):
            first, second = _load_pair(tab_ref, idx_t[2 * p], idx_t[2 * p + 1])
            cp = jnp.where(low,
                           jnp.broadcast_to(cb_ref[2 * p:2 * p + 1, :], (SUBLANES, LANES)),
                           jnp.broadcast_to(cb_ref[2 * p + 1:2 * p + 2, :], (SUBLANES, LANES)))
            lo[p % n_acc] = lo[p % n_acc] + cp * first
            hi[p % n_acc] = hi[p % n_acc] + cp * second
        lo = (lo[0] + lo[1]) + (lo[2] + lo[3])
        hi = (hi[0] + hi[1]) + (hi[2] + hi[3])
        o_ref[t] = jnp.concatenate([lo[0:ROW_UNITS, :] + lo[ROW_UNITS:, :],
                                    hi[0:ROW_UNITS, :] + hi[ROW_UNITS:, :]], axis=0)

    def pair(i, carry):
        prepare(2 * i + 1, cb1_ref)
        tok(2 * i, cb0_ref)
        prepare(jnp.minimum(2 * i + 2, tb - 1), cb0_ref)
        tok(2 * i + 1, cb1_ref)
        return carry

    prepare(0, cb0_ref)
    lax.fori_loop(0, tb // 2, pair, 0)


def _peer_out(eidx, coef, tab_v, tb=64):
    n = eidx.shape[0]
    chunks = 2 * ROW_UNITS
    out = pl.pallas_call(
        functools.partial(_peer_out_kernel, tb=tb),
        grid=(n // tb,),
        in_specs=[pl.BlockSpec((tb * PEER_SLOTS,), lambda i: (i,), memory_space=pltpu.SMEM),
                  pl.BlockSpec(memory_space=pltpu.VMEM),
                  pl.BlockSpec((tb, PEER_SLOTS), lambda i: (i, 0))],
        out_specs=pl.BlockSpec((tb, chunks, LANES), lambda i: (i, 0, 0)),
        out_shape=jax.ShapeDtypeStruct((n, chunks, LANES), F32),
        scratch_shapes=[pltpu.VMEM((PEER_SLOTS, PEER_SLOTS), F32)] * 2,
        compiler_params=_cp(("arbitrary",)),
        name="peer_out",
    )(_row_offsets(eidx), tab_v, coef)
    return out.reshape(n, chunks * LANES)


def _final_kernel(x1_ref, g2_ref, p_ref, y_ref):
    y_ref[...] = x1_ref[...] + g2_ref[...] * p_ref[...]


def _final(x1, gate2, peer, per_token, tm=512):
    bsz, t, d = x1.shape
    row = lambda b, i: (b, i, 0)
    mod_spec = (pl.BlockSpec((None, tm, d), row) if per_token
                else pl.BlockSpec((None, 1, d), lambda b, i: (b, 0, 0)))
    full = pl.BlockSpec((None, tm, d), row)
    return pl.pallas_call(
        _final_kernel,
        grid=(bsz, t // tm),
        in_specs=[full, mod_spec, full],
        out_specs=full,
        out_shape=jax.ShapeDtypeStruct((bsz, t, d), F32),
        compiler_params=_cp(("parallel", "parallel")),
        name="final",
    )(x1, gate2, peer)


def _prep_inproj(w_in, b_f, b_glu, q_norm_g, k_norm_g, tm):
    d = D_ATT
    nf = 3 * d + N_HEADS
    w = jnp.concatenate([w_in[:, :3 * d], w_in[:, nf:]], axis=1).astype(BF16)
    wf = jnp.pad(w_in[:, 3 * d:nf], ((0, 0), (0, LANES - N_HEADS)))
    wfh = wf.astype(BF16)
    wfl = (wf - wfh.astype(F32)).astype(BF16)
    bfp = jnp.pad(b_f, (0, LANES - N_HEADS)).reshape(1, LANES)
    qg = jnp.tile(q_norm_g, N_HEADS).reshape(1, d)
    kg = jnp.tile(k_norm_g, N_HEADS).reshape(1, d)
    hid = jnp.arange(d) // HEAD_DIM
    ones_bd = (hid[:, None] == hid[None, :]).astype(BF16)
    tri = jnp.tril(jnp.ones((tm, tm), BF16))
    return (w, wfh, wfl, bfp, b_glu.reshape(1, -1), qg, kg, ones_bd, tri)


def _peer_ffn(h2, wq_t, skh, skl, tab_u, tab_v):
    eidx, g = _route(h2, wq_t, skh, skl)
    coef = _peer_act(eidx, tab_u, h2, g)
    return _peer_out(eidx, coef, tab_v)


def kernel(x_prompt, x_sample, c_prompt, c_sample, cache_k, cache_v, cache_logf, state_conv,
           page_table, norm1_g, norm2_g, w_ada, b_ada, w_in, b_f, b_glu, q_norm_g, k_norm_g,
           conv_w, conv_b, conv_ln_g, conv_ln_b, w_o, peer_w_query, peer_sub_keys, peer_u,
           peer_v):
    depth = w_ada.shape[0]
    bsz, seq, d = x_prompt.shape
    n_dec, s_new, _ = x_sample.shape
    tm = 256
    xp, xs = x_prompt, x_sample.reshape(1, n_dec * s_new, d)
    outs = [[] for _ in range(8)]
    for l in range(depth):
        n_c = bsz + n_dec
        pad = (-n_c) % SUBLANES
        c_all = jnp.pad(jnp.concatenate([c_prompt, c_sample], axis=0), ((0, pad), (0, 0)))
        mod = _adaln(c_all, w_ada[l], b_ada[l])
        mod_p = [m.reshape(bsz, 1, d) for m in jnp.split(mod[:bsz], 6, axis=-1)]
        mod_s = [jnp.repeat(m, s_new, axis=0).reshape(1, n_dec * s_new, d)
                 for m in jnp.split(mod[bsz:n_c], 6, axis=-1)]

        wts = _prep_inproj(w_in[l], b_f[l], b_glu[l], q_norm_g[l], k_norm_g[l], tm)
        g1 = norm1_g[l].reshape(1, d)
        n2 = norm2_g[l].reshape(1, d)
        cw = conv_w[l]
        cb, lg, lb = (a[l].reshape(1, -1) for a in (conv_b, conv_ln_g, conv_ln_b))
        woa = w_o[l][:D_ATT].astype(BF16)
        woc = w_o[l][D_ATT:].astype(BF16)
        wq_t = peer_w_query[l].T.astype(BF16)
        sk = peer_sub_keys[l].reshape(PEER_HEADS * 2, PEER_N_KEYS, PEER_HALF)
        skh = sk.astype(BF16)
        skl = (sk - skh.astype(F32)).astype(BF16)
        tab_u = _pack_table(peer_u[l])
        tab_v = _pack_table(peer_v[l])

        sh1, sc1, gt1, sh2, sc2, gt2 = mod_p
        qb, k, v, kb, vb, lf, fc, u = _inproj(xp, sh1, sc1, g1, wts, per_token=False, tm=tm)
        a_out = _prompt_attention(qb, kb, vb, fc)
        c_out = _conv_prompt(u, cw, cb, lg, lb)
        x1, h2 = _tail(xp, a_out, c_out, gt1, sh2, sc2, n2, woa, woc, per_token=False, tm=tm)
        peer = _peer_ffn(h2.reshape(bsz * seq, d), wq_t, skh, skl, tab_u, tab_v)
        xp = _final(x1, gt2, peer.reshape(bsz, seq, d), per_token=False)
        outs[0].append(k.reshape(bsz, seq, N_HEADS, HEAD_DIM))
        outs[1].append(v.reshape(bsz, seq, N_HEADS, HEAD_DIM))
        outs[2].append(lf)
        outs[3].append(u[:, seq - (CONV_WIDTH - 1):, :])

        sh1, sc1, gt1, sh2, sc2, gt2 = mod_s
        qb, k, v, kb, vb, lf, fc, u = _inproj(xs, sh1, sc1, g1, wts, per_token=True, tm=tm)
        shp = (n_dec, s_new, D_ATT)
        a_out = _sample_attention(qb.reshape(shp).astype(F32), k.reshape(shp), v.reshape(shp),
                                  lf.reshape(n_dec, s_new, N_HEADS), cache_k[l], cache_v[l],
                                  cache_logf[l], page_table)
        u_s = u.reshape(shp)
        c_out = _conv_sample(state_conv[l].transpose(1, 0, 2), u_s.transpose(1, 0, 2),
                             cw, cb, lg, lb).transpose(1, 0, 2)
        x1, h2 = _tail(xs, a_out.reshape(1, -1, D_ATT), c_out.reshape(1, -1, D_ATT), gt1, sh2,
                       sc2, n2, woa, woc, per_token=True, tm=tm)
        peer = _peer_ffn(h2.reshape(n_dec * s_new, d), wq_t, skh, skl, tab_u, tab_v)
        xs = _final(x1, gt2, peer.reshape(1, n_dec * s_new, d), per_token=True)
        outs[4].append(k.reshape(n_dec, s_new, N_HEADS, HEAD_DIM))
        outs[5].append(v.reshape(n_dec, s_new, N_HEADS, HEAD_DIM))
        outs[6].append(lf.reshape(n_dec, s_new, N_HEADS))
        outs[7].append(jnp.concatenate([state_conv[l], u_s], axis=1)[:, -(CONV_WIDTH - 1):])
    st = [jnp.stack(o) for o in outs]
    return (xp, xs.reshape(n_dec, s_new, d), st[0], st[1], st[2], st[3], st[4], st[5], st[6],
            st[7])
```

```python
import functools

import jax
import jax.numpy as jnp
from jax import lax
from jax.experimental import pallas as pl
from jax.experimental.pallas import tpu as pltpu

F32 = jnp.float32
BF16 = jnp.bfloat16
I32 = jnp.int32

EPS = 1e-6
N_HEADS = 8
HEAD_DIM = 64
D_ATT = N_HEADS * HEAD_DIM
CONV_WIDTH = 31
PAGE_SIZE = 128
PEER_HEADS = 8
PEER_N_KEYS = 128
PEER_HALF = 128
PEER_TOPK = 16
PEER_SLOTS = PEER_HEADS * PEER_TOPK
ROW_UNITS = 4
LANES = 128
SUBLANES = 8
NEG = -1e30
VMEM_LIMIT = 56 * 1024 * 1024


def _cp(sem, vmem=VMEM_LIMIT):
    return pltpu.CompilerParams(dimension_semantics=sem, vmem_limit_bytes=vmem)


def _dot(a, b):
    return jnp.dot(a, b, preferred_element_type=F32)


def _dot_nt(a, b):
    return lax.dot_general(a, b, (((1,), (1,)), ((), ())), preferred_element_type=F32)


def _split2(a):
    hi = a.astype(BF16)
    lo = (a - hi.astype(F32)).astype(BF16)
    return hi, lo


def _split3(a):
    hi = a.astype(BF16)
    r = a - hi.astype(F32)
    mid = r.astype(BF16)
    lo = (r - mid.astype(F32)).astype(BF16)
    return hi, mid, lo


def _log_sigmoid(x):
    return jnp.minimum(x, 0.0) - jnp.log1p(jnp.exp(-jnp.abs(x)))


def _ada_kernel(c_ref, w_ref, b_ref, o_ref):
    ch, cl = _split2(c_ref[...])
    wh, wl = _split2(w_ref[...])
    o_ref[...] = _dot(ch, wh) + (_dot(ch, wl) + _dot(cl, wh)) + b_ref[...]


def _adaln(c, w_ada, b_ada):
    r, d = c.shape
    n = w_ada.shape[1]
    tn = 512
    return pl.pallas_call(
        _ada_kernel,
        grid=(n // tn,),
        in_specs=[pl.BlockSpec((r, d), lambda j: (0, 0)),
                  pl.BlockSpec((d, tn), lambda j: (0, j)),
                  pl.BlockSpec((1, tn), lambda j: (0, j))],
        out_specs=pl.BlockSpec((r, tn), lambda j: (0, j)),
        out_shape=jax.ShapeDtypeStruct((r, n), F32),
        compiler_params=_cp(("parallel",)),
        name="adaln",
    )(c, w_ada, b_ada.reshape(1, n))


def _inproj_kernel(x_ref, sh_ref, sc_ref, g1_ref, w_ref, wfh_ref, wfl_ref, bf_ref, bglu_ref,
                   qg_ref, kg_ref, ones_ref, tri_ref,
                   q_ref, k_ref, v_ref, kb_ref, vb_ref, lf_ref, fc_ref, u_ref, carry_ref):
    t = pl.program_id(1)

    @pl.when(t == 0)
    def _():
        carry_ref[...] = jnp.zeros_like(carry_ref)

    x = x_ref[...]
    ms = jnp.mean(x * x, axis=-1, keepdims=True)
    h = x * lax.rsqrt(ms + EPS) * g1_ref[...]
    h = h * (1.0 + sc_ref[...]) + sh_ref[...]
    hh, hl = _split2(h)
    z = _dot(hh, w_ref[...])
    fg = _dot(hh, wfh_ref[...]) + (_dot(hh, wfl_ref[...]) + _dot(hl, wfh_ref[...]))
    d = D_ATT
    q = z[:, 0:d]
    k = z[:, d:2 * d]
    v = z[:, 2 * d:3 * d]
    a = z[:, 3 * d:4 * d] + bglu_ref[:, 0:d]
    b = z[:, 4 * d:5 * d] + bglu_ref[:, d:2 * d]
    qs = _dot((q * q).astype(BF16), ones_ref[...])
    ks = _dot((k * k).astype(BF16), ones_ref[...])
    qn = q * lax.rsqrt(qs * (1.0 / HEAD_DIM) + EPS) * qg_ref[...]
    kn = k * lax.rsqrt(ks * (1.0 / HEAD_DIM) + EPS) * kg_ref[...]
    q_ref[...] = (qn * (HEAD_DIM ** -0.5)).astype(BF16)
    k_ref[...] = kn
    kb_ref[...] = kn.astype(BF16)
    v_ref[...] = v
    vb_ref[...] = v.astype(BF16)
    u_ref[...] = a * jax.nn.sigmoid(b)
    lf = _log_sigmoid(fg + bf_ref[...])
    lf_ref[...] = lf[:, 0:N_HEADS]
    l1, l2, l3 = _split3(lf)
    tri = tri_ref[...]
    fc = _dot(tri, l1) + (_dot(tri, l2) + _dot(tri, l3)) + carry_ref[...]
    fc_ref[...] = fc[:, 0:N_HEADS]
    tm = fc.shape[0]
    carry_ref[...] = fc[tm - 1:tm, :]


def _inproj(x, shift, scale, g1, wts, per_token, tm=256):
    bsz, t, d = x.shape
    w, wfh, wfl, bfp, bglu, qg, kg, ones_bd, tri = wts
    row = lambda b, i: (b, i, 0)
    const2 = lambda b, i: (0, 0)
    mod_spec = (pl.BlockSpec((None, tm, d), row) if per_token
                else pl.BlockSpec((None, 1, d), lambda b, i: (b, 0, 0)))
    full = lambda arr: pl.BlockSpec(arr.shape, const2)
    outs = [
        jax.ShapeDtypeStruct((bsz, t, D_ATT), BF16),
        jax.ShapeDtypeStruct((bsz, t, D_ATT), F32),
        jax.ShapeDtypeStruct((bsz, t, D_ATT), F32),
        jax.ShapeDtypeStruct((bsz, t, D_ATT), BF16),
        jax.ShapeDtypeStruct((bsz, t, D_ATT), BF16),
        jax.ShapeDtypeStruct((bsz, t, N_HEADS), F32),
        jax.ShapeDtypeStruct((bsz, t, N_HEADS), F32),
        jax.ShapeDtypeStruct((bsz, t, D_ATT), F32),
    ]
    ospec = lambda s: pl.BlockSpec((None, tm, s.shape[2]), row)
    return pl.pallas_call(
        _inproj_kernel,
        grid=(bsz, t // tm),
        in_specs=[pl.BlockSpec((None, tm, d), row), mod_spec, mod_spec, full(g1), full(w),
                  full(wfh), full(wfl), full(bfp), full(bglu), full(qg), full(kg),
                  full(ones_bd), full(tri)],
        out_specs=[ospec(s) for s in outs],
        out_shape=outs,
        scratch_shapes=[pltpu.VMEM((1, LANES), F32)],
        compiler_params=_cp(("parallel", "arbitrary")),
        name="inproj",
    )(x, shift, scale, g1, w, wfh, wfl, bfp, bglu, qg, kg, ones_bd, tri)


def _attn_kernel(q_ref, k_ref, v_ref, f_ref, o_ref, acc_ref, *, tq, tk):
    qi = pl.program_id(2)
    q = q_ref[...]
    lane = lax.broadcasted_iota(I32, (1, LANES), 1)
    zero = jnp.zeros_like(q)
    qm = [jnp.where(lane < HEAD_DIM, q, zero), jnp.where(lane >= HEAD_DIM, q, zero)]
    acc_ref[...] = jnp.zeros_like(acc_ref)
    rows = lax.broadcasted_iota(I32, (tq, tk), 0)
    cols = lax.broadcasted_iota(I32, (tq, tk), 1)

    def step(j, carry, diag_off=None):
        start = pl.multiple_of(j * tk, tk)
        kc = k_ref[pl.ds(start, tk), :]
        vc = v_ref[pl.ds(start, tk), :]
        fr = f_ref[j]
        new = []
        for hh in range(2):
            m, l = carry[2 * hh], carry[2 * hh + 1]
            s = _dot_nt(qm[hh], kc) - fr[hh:hh + 1, :]
            if diag_off is not None:
                s = jnp.where(cols + diag_off <= rows, s, NEG)
            m_new = jnp.maximum(m, jnp.max(s, axis=1, keepdims=True))
            alpha = jnp.exp(m - m_new)
            p = jnp.exp(s - m_new)
            l_new = alpha * l + jnp.sum(p, axis=1, keepdims=True)
            acc_ref[hh] = alpha * acc_ref[hh] + _dot(p.astype(BF16), vc)
            new += [m_new, l_new]
        return tuple(new)

    init = tuple(jnp.full((tq, 1), NEG, F32) if i % 2 == 0 else jnp.zeros((tq, 1), F32)
                 for i in range(4))
    n_diag = tq // tk
    carry = lax.fori_loop(0, qi * n_diag, step, init)
    for dd in range(n_diag):
        carry = step(qi * n_diag + dd, carry, diag_off=dd * tk)
    o0 = acc_ref[0] / carry[1]
    o1 = acc_ref[1] / carry[3]
    o_ref[...] = jnp.where(lane < HEAD_DIM, o0, o1).astype(o_ref.dtype)


def _prompt_attention(qb, kb, vb, fcum, tq=1024, tk=256):
    bsz, t, _ = qb.shape
    n_pairs = N_HEADS // 2
    nk = t // tk
    f_rows = fcum.transpose(0, 2, 1).reshape(bsz, n_pairs, 2, nk, tk).transpose(0, 1, 3, 2, 4)
    return pl.pallas_call(
        functools.partial(_attn_kernel, tq=tq, tk=tk),
        grid=(bsz, n_pairs, t // tq),
        in_specs=[pl.BlockSpec((None, tq, LANES), lambda b, p, i: (b, i, p)),
                  pl.BlockSpec((None, t, LANES), lambda b, p, i: (b, 0, p)),
                  pl.BlockSpec((None, t, LANES), lambda b, p, i: (b, 0, p)),
                  pl.BlockSpec((None, None, nk, 2, tk), lambda b, p, i: (b, p, 0, 0, 0))],
        out_specs=pl.BlockSpec((None, tq, LANES), lambda b, p, i: (b, i, p)),
        out_shape=jax.ShapeDtypeStruct((bsz, t, D_ATT), BF16),
        scratch_shapes=[pltpu.VMEM((2, tq, LANES), F32)],
        compiler_params=_cp(("parallel", "parallel", "arbitrary")),
        name="prompt_attn",
    )(qb, kb, vb, f_rows)


def _sattn_kernel(pt_ref, q_ref, *refs, n_new, n_pg):
    ck = refs[0:n_pg]
    cv = refs[n_pg:2 * n_pg]
    clf = refs[2 * n_pg:3 * n_pg]
    kn_ref, vn_ref, lfn_ref, triu_ref, o_ref, m_ref, l_ref, acc_ref, fcar_ref = refs[3 * n_pg:]
    p = pl.program_id(1)
    nq = n_new * N_HEADS
    lane = lax.broadcasted_iota(I32, (N_HEADS, D_ATT), 1)
    hrow = lax.broadcasted_iota(I32, (N_HEADS, D_ATT), 0)
    hmask = (lane // HEAD_DIM == hrow)

    @pl.when(p == 0)
    def _():
        m_ref[...] = jnp.full_like(m_ref, NEG)
        l_ref[...] = jnp.zeros_like(l_ref)
        acc_ref[...] = jnp.zeros_like(acc_ref)
        fcar_ref[...] = jnp.zeros_like(fcar_ref)

    q = q_ref[...]
    zq = jnp.zeros((N_HEADS, D_ATT), F32)
    qexp = jnp.concatenate(
        [jnp.where(hmask, jnp.broadcast_to(q[t:t + 1, :], (N_HEADS, D_ATT)), zq)
         for t in range(n_new)], axis=0).astype(BF16)

    def update(scores, values, transposed):
        m = m_ref[...]
        m_new = m
        for s in scores:
            m_new = jnp.maximum(m_new, jnp.max(s, axis=1, keepdims=True))
        alpha = jnp.exp(m - m_new)
        l_new = alpha * l_ref[...]
        acc = alpha * acc_ref[...]
        for s, vmat in zip(scores, values):
            pr = jnp.exp(s - m_new)
            l_new = l_new + jnp.sum(pr, axis=1, keepdims=True)
            pb = pr.astype(BF16)
            acc = acc + (_dot_nt(pb, vmat) if transposed else _dot(pb, vmat))
        l_ref[...] = l_new
        acc_ref[...] = acc
        m_ref[...] = m_new

    l1, l2, l3 = _split3(jnp.concatenate([r[...] for r in clf], axis=0))
    tri = triu_ref[...]
    local = _dot(l1, tri) + (_dot(l2, tri) + _dot(l3, tri))
    fcar = fcar_ref[...]
    scores = []
    for g in range(n_pg):
        fp = local[g * N_HEADS:(g + 1) * N_HEADS, :] + fcar
        fcar = jnp.broadcast_to(fp[:, PAGE_SIZE - 1:PAGE_SIZE], fcar.shape)
        scores.append(_dot(qexp, ck[g][...].astype(BF16))
                      - jnp.concatenate([fp] * n_new, axis=0))
    fcar_ref[...] = fcar
    update(scores, [cv[g][...].astype(BF16) for g in range(n_pg)], transposed=True)

    @pl.when(p == pl.num_programs(1) - 1)
    def _():
        lfn = lfn_ref[...]
        cols = []
        run = fcar[:, 0:1]
        for t in range(n_new):
            run = run + lfn[:, t:t + 1]
            cols.append(run)
        pad = SUBLANES - n_new
        fnew = jnp.concatenate(cols + [jnp.zeros((N_HEADS, pad), F32)], axis=1)
        zk = jnp.zeros((pad, D_ATT), F32)
        kn = jnp.concatenate([kn_ref[...], zk], axis=0).astype(BF16)
        vn = jnp.concatenate([vn_ref[...], zk], axis=0).astype(BF16)
        s2 = _dot_nt(qexp, kn) - jnp.concatenate([fnew] * n_new, axis=0)
        r = lax.broadcasted_iota(I32, (nq, SUBLANES), 0) // N_HEADS
        c = lax.broadcasted_iota(I32, (nq, SUBLANES), 1)
        update([jnp.where(c <= r, s2, NEG)], [vn], transposed=False)
        out = acc_ref[...] / l_ref[...]
        zo = jnp.zeros((N_HEADS, D_ATT), F32)
        rows = [jnp.sum(jnp.where(hmask, out[t * N_HEADS:(t + 1) * N_HEADS, :], zo),
                        axis=0, keepdims=True) for t in range(n_new)]
        o_ref[...] = jnp.concatenate(rows, axis=0)


def _sample_attention(qb, kn, vn, lfn, cache_k, cache_v, cache_logf, page_table, n_pg=8):
    n, s_new, _ = qb.shape
    n_pages = page_table.shape[1]
    n_phys = cache_k.shape[0]
    nq = s_new * N_HEADS
    ck = cache_k.transpose(0, 2, 3, 1).reshape(n_phys, D_ATT, PAGE_SIZE)
    cv = cache_v.transpose(0, 2, 3, 1).reshape(n_phys, D_ATT, PAGE_SIZE)
    clf = cache_logf.transpose(0, 2, 1)
    lfn_t = lfn.transpose(0, 2, 1)
    triu = jnp.triu(jnp.ones((PAGE_SIZE, PAGE_SIZE), BF16))
    seq = lambda i, p, pt: (i, 0, 0)
    tok_spec = pl.BlockSpec((None, s_new, D_ATT), seq)

    def page(g):
        return lambda i, p, pt: (pt[i * n_pages + p * n_pg + g], 0, 0)

    kv_specs = [pl.BlockSpec((None, D_ATT, PAGE_SIZE), page(g)) for g in range(n_pg)]
    lf_specs = [pl.BlockSpec((None, N_HEADS, PAGE_SIZE), page(g)) for g in range(n_pg)]
    gs = pltpu.PrefetchScalarGridSpec(
        num_scalar_prefetch=1,
        grid=(n, n_pages // n_pg),
        in_specs=[tok_spec] + kv_specs + kv_specs + lf_specs + [
            tok_spec, tok_spec,
            pl.BlockSpec((None, N_HEADS, s_new), seq),
            pl.BlockSpec(triu.shape, lambda i, p, pt: (0, 0))],
        out_specs=tok_spec,
        scratch_shapes=[pltpu.VMEM((nq, 1), F32), pltpu.VMEM((nq, 1), F32),
                        pltpu.VMEM((nq, D_ATT), F32),
                        pltpu.VMEM((N_HEADS, PAGE_SIZE), F32)],
    )
    return pl.pallas_call(
        functools.partial(_sattn_kernel, n_new=s_new, n_pg=n_pg),
        grid_spec=gs,
        out_shape=jax.ShapeDtypeStruct((n, s_new, D_ATT), F32),
        compiler_params=_cp(("parallel", "arbitrary")),
        name="sample_attn",
    )(page_table.reshape(-1), qb, *([ck] * n_pg), *([cv] * n_pg), *([clf] * n_pg),
      kn, vn, lfn_t, triu)


def _ln_swish(y, g, b):
    mu = jnp.mean(y, axis=-1, keepdims=True)
    yc = y - mu
    var = jnp.mean(yc * yc, axis=-1, keepdims=True)
    yn = yc * lax.rsqrt(var + EPS) * g + b
    return yn * jax.nn.sigmoid(yn)


def _conv_kernel(u_ref, cw_ref, cb_ref, g_ref, b_ref, o_ref, ext_ref, *, tm, hist):
    t = pl.program_id(1)

    @pl.when(t == 0)
    def _():
        ext_ref[0:hist, :] = jnp.zeros((hist, ext_ref.shape[1]), F32)

    ext_ref[hist:hist + tm, :] = u_ref[...]
    off = hist - (CONV_WIDTH - 1)
    y = jnp.zeros(u_ref.shape, F32) + cb_ref[...]
    for w in range(CONV_WIDTH):
        y = y + ext_ref[off + w:off + w + tm, :] * cw_ref[w:w + 1, :]
    o_ref[...] = _ln_swish(y, g_ref[...], b_ref[...]).astype(o_ref.dtype)
    ext_ref[0:hist, :] = ext_ref[tm:tm + hist, :]


def _conv_prompt(u, cw, cb, g, b, tm=512):
    bsz, t, c = u.shape
    hist = 32
    row = lambda bb, i: (bb, i, 0)
    const = lambda bb, i: (0, 0)
    return pl.pallas_call(
        functools.partial(_conv_kernel, tm=tm, hist=hist),
        grid=(bsz, t // tm),
        in_specs=[pl.BlockSpec((None, tm, c), row), pl.BlockSpec(cw.shape, const),
                  pl.BlockSpec((1, c), const), pl.BlockSpec((1, c), const),
                  pl.BlockSpec((1, c), const)],
        out_specs=pl.BlockSpec((None, tm, c), row),
        out_shape=jax.ShapeDtypeStruct((bsz, t, c), BF16),
        scratch_shapes=[pltpu.VMEM((hist + tm, c), F32)],
        compiler_params=_cp(("parallel", "arbitrary")),
        name="conv_prompt",
    )(u, cw, cb, g, b)


def _conv_step_kernel(st_ref, u_ref, cw_ref, cb_ref, g_ref, b_ref, o_ref, *, n_hist, n_new):
    def ext(i):
        return st_ref[i] if i < n_hist else u_ref[i - n_hist]

    for t in range(n_new):
        y = jnp.zeros(o_ref.shape[1:], F32) + cb_ref[...]
        for w in range(CONV_WIDTH):
            y = y + ext(t + w) * cw_ref[w:w + 1, :]
        o_ref[t] = _ln_swish(y, g_ref[...], b_ref[...]).astype(o_ref.dtype)


def _conv_sample(state_t, u_t, cw, cb, g, b):
    n_hist, n, c = state_t.shape
    n_new = u_t.shape[0]
    return pl.pallas_call(
        functools.partial(_conv_step_kernel, n_hist=n_hist, n_new=n_new),
        out_shape=jax.ShapeDtypeStruct((n_new, n, c), BF16),
        compiler_params=pltpu.CompilerParams(vmem_limit_bytes=VMEM_LIMIT),
        name="conv_sample",
    )(state_t, u_t, cw, cb, g, b)


def _tail_kernel(x_ref, a_ref, c_ref, g1_ref, sh_ref, sc_ref, n2_ref, woa_ref, woc_ref,
                 x1_ref, h2_ref):
    proj = (_dot(a_ref[...].astype(BF16), woa_ref[...])
            + _dot(c_ref[...].astype(BF16), woc_ref[...]))
    x1 = x_ref[...] + g1_ref[...] * proj
    x1_ref[...] = x1
    ms = jnp.mean(x1 * x1, axis=-1, keepdims=True)
    h2 = x1 * lax.rsqrt(ms + EPS) * n2_ref[...]
    h2_ref[...] = h2 * (1.0 + sc_ref[...]) + sh_ref[...]


def _tail(x, attn, conv, gate1, shift2, scale2, n2g, woa, woc, per_token, tm=256):
    bsz, t, d = x.shape
    row = lambda b, i: (b, i, 0)
    const2 = lambda b, i: (0, 0)
    mod_spec = (pl.BlockSpec((None, tm, d), row) if per_token
                else pl.BlockSpec((None, 1, d), lambda b, i: (b, 0, 0)))
    half = pl.BlockSpec((None, tm, D_ATT), row)
    full = pl.BlockSpec((None, tm, d), row)
    return pl.pallas_call(
        _tail_kernel,
        grid=(bsz, t // tm),
        in_specs=[full, half, half, mod_spec, mod_spec, mod_spec,
                  pl.BlockSpec(n2g.shape, const2), pl.BlockSpec(woa.shape, const2),
                  pl.BlockSpec(woc.shape, const2)],
        out_specs=[full, full],
        out_shape=[jax.ShapeDtypeStruct((bsz, t, d), F32)] * 2,
        compiler_params=_cp(("parallel", "parallel")),
        name="tail",
    )(x, attn, conv, gate1, shift2, scale2, n2g, woa, woc)


def _top_rows(s, order, payload, k):
    big = float(2 ** 24)
    vals, outs = [], []
    for _ in range(k):
        m = jnp.max(s, axis=0, keepdims=True)
        pos = jnp.min(jnp.where(s == m, order, big), axis=0, keepdims=True)
        hit = order == pos
        vals.append(m)
        if payload is None:
            outs.append(pos)
        else:
            outs.append(jnp.sum(jnp.where(hit, payload, 0), axis=0, keepdims=True))
        s = jnp.where(hit, -jnp.inf, s)
    return jnp.concatenate(vals, axis=0), jnp.concatenate(outs, axis=0)


def _pair_groups():
    groups = [(0, PEER_TOPK, PEER_TOPK)]
    for a in range(1, SUBLANES):
        groups.append((a, SUBLANES, PEER_TOPK // (a + 1)))
    groups.append((None, SUBLANES, SUBLANES))
    return groups


def _route_kernel(h_ref, wq_ref, skh_ref, skl_ref, e_ref, g_ref):
    hb = h_ref[...].astype(BF16)
    tb = hb.shape[0]
    key_order = lax.broadcasted_iota(I32, (PEER_N_KEYS, tb), 0).astype(F32)
    groups = _pair_groups()
    orders, masks = [], []
    for a, rows, valid in groups:
        r = lax.broadcasted_iota(I32, (rows, tb), 0)
        flat = (a * PEER_TOPK + r) if a is not None else (SUBLANES + r) * PEER_TOPK
        orders.append(flat.astype(F32))
        masks.append(r < valid)
    pair_order = jnp.concatenate(orders, axis=0)
    pair_valid = jnp.concatenate(masks, axis=0)
    e_rows, g_rows = [], []
    for hd in range(PEER_HEADS):
        tops = []
        for half in range(2):
            i = hd * 2 + half
            wq = wq_ref[i * PEER_HALF:(i + 1) * PEER_HALF, :]
            qt = _dot_nt(wq, hb)
            qh, ql = _split2(qt)
            skh = skh_ref[i]
            s = _dot(skh, qh) + (_dot(skh, ql) + _dot(skl_ref[i], qh))
            vals, pos = _top_rows(s, key_order, None, PEER_TOPK)
            tops.append((vals, pos.astype(I32)))
        (s1, i1), (s2, i2) = tops
        cand, cidx = [], []
        for a, rows, _ in groups:
            if a is not None:
                cand.append(s1[a:a + 1, :] + s2[0:rows, :])
                cidx.append(i1[a:a + 1, :] * PEER_N_KEYS + i2[0:rows, :])
            else:
                cand.append(s1[SUBLANES:, :] + s2[0:1, :])
                cidx.append(i1[SUBLANES:, :] * PEER_N_KEYS + i2[0:1, :])
        cand = jnp.where(pair_valid, jnp.concatenate(cand, axis=0), -jnp.inf)
        top, eidx = _top_rows(cand, pair_order, jnp.concatenate(cidx, axis=0), PEER_TOPK)
        ex = jnp.exp(top - top[0:1, :])
        g_rows.append(ex / jnp.sum(ex, axis=0, keepdims=True))
        e_rows.append(eidx)
    e_ref[...] = jnp.concatenate(e_rows, axis=0).T
    g_ref[...] = jnp.concatenate(g_rows, axis=0).T


def _route(h2, wq_t, skh, skl, tb=128):
    n, d = h2.shape
    const2 = lambda i: (0, 0)
    const3 = lambda i: (0, 0, 0)
    return pl.pallas_call(
        _route_kernel,
        grid=(n // tb,),
        in_specs=[pl.BlockSpec((tb, d), lambda i: (i, 0)), pl.BlockSpec(wq_t.shape, const2),
                  pl.BlockSpec(skh.shape, const3), pl.BlockSpec(skl.shape, const3)],
        out_specs=[pl.BlockSpec((tb, PEER_SLOTS), lambda i: (i, 0))] * 2,
        out_shape=[jax.ShapeDtypeStruct((n, PEER_SLOTS), I32),
                   jax.ShapeDtypeStruct((n, PEER_SLOTS), F32)],
        compiler_params=_cp(("parallel",)),
        name="peer_route",
    )(h2, wq_t, skh, skl)


def _pack_table(tab):
    e, d = tab.shape
    bits = lax.bitcast_convert_type(tab.astype(BF16), jnp.uint16).astype(jnp.uint32)
    words = bits[:, :d // 2] | (bits[:, d // 2:] << 16)
    return jnp.pad(words.reshape(e * ROW_UNITS, LANES), ((ROW_UNITS, ROW_UNITS), (0, 0)))


def _row_offsets(eidx):
    return (eidx * ROW_UNITS).reshape(-1)


def _row_offset_pairs(eidx):
    off = eidx * ROW_UNITS
    return (off[:, 0::2] | (off[:, 1::2] << 16)).reshape(-1)


def _load_pair(tab_ref, off_even, off_odd):
    low = lax.broadcasted_iota(I32, (SUBLANES, LANES), 0) < ROW_UNITS
    w = jnp.where(low, tab_ref[pl.ds(off_even + ROW_UNITS, SUBLANES), :],
                  tab_ref[pl.ds(off_odd, SUBLANES), :])
    first = lax.bitcast_convert_type(w << 16, F32)
    second = lax.bitcast_convert_type(w & jnp.uint32(0xFFFF0000), F32)
    return first, second


def _peer_act_kernel(idx_ref, tab_ref, h_ref, g_ref, c_ref, ps0_ref, ps1_ref, ps2_ref, ps3_ref,
                     act_ref, *, tb):
    def gather(t, ps_ref):
        hv = h_ref[t]
        h_first = jnp.concatenate([hv[0:ROW_UNITS, :]] * 2, axis=0)
        h_second = jnp.concatenate([hv[ROW_UNITS:, :]] * 2, axis=0)
        idx_t = idx_ref.at[pl.ds(t * (PEER_SLOTS // 2), PEER_SLOTS // 2)]
        for p in range(PEER_SLOTS // 2):
            packed = idx_t[p]
            first, second = _load_pair(tab_ref, packed & 0xFFFF,
                                       lax.shift_right_logical(packed, 16))
            ps_ref[p * SUBLANES:(p + 1) * SUBLANES, :] = first * h_first + second * h_second

    def reduce(t, ps_ref):
        s = ps_ref[pl.ds(0, PEER_SLOTS, stride=ROW_UNITS), :]
        for r in range(1, ROW_UNITS):
            s = s + ps_ref[pl.ds(r, PEER_SLOTS, stride=ROW_UNITS), :]
        act_ref[pl.ds(t, 1), :] = jnp.sum(s.T, axis=0, keepdims=True)

    ps = (ps0_ref, ps1_ref, ps2_ref, ps3_ref)

    per_trip = 4

    def trip(i, carry):
        for k in range(per_trip):
            gather(per_trip * i + k, ps[k % 4])
            reduce(jnp.maximum(per_trip * i + k - 2, 0), ps[(k + 2) % 4])
        return carry

    ps2_ref[...] = jnp.zeros_like(ps2_ref)
    ps3_ref[...] = jnp.zeros_like(ps3_ref)
    lax.fori_loop(0, tb // per_trip, trip, 0)
    reduce(tb - 2, ps2_ref)
    reduce(tb - 1, ps3_ref)
    act = act_ref[...]
    gelu = 0.5 * act * (1.0 + lax.erf(act * (2.0 ** -0.5)))
    c_ref[...] = g_ref[...] * gelu


def _peer_act(eidx, tab_u, h2, g, tb=64):
    n = eidx.shape[0]
    d = h2.shape[1]
    hc = h2.reshape(n, d // LANES, LANES)
    return pl.pallas_call(
        functools.partial(_peer_act_kernel, tb=tb),
        grid=(n // tb,),
        in_specs=[pl.BlockSpec((tb * PEER_SLOTS // 2,), lambda i: (i,), memory_space=pltpu.SMEM),
                  pl.BlockSpec(memory_space=pltpu.VMEM),
                  pl.BlockSpec((tb, d // LANES, LANES), lambda i: (i, 0, 0)),
                  pl.BlockSpec((tb, PEER_SLOTS), lambda i: (i, 0))],
        out_specs=pl.BlockSpec((tb, PEER_SLOTS), lambda i: (i, 0)),
        out_shape=jax.ShapeDtypeStruct((n, PEER_SLOTS), F32),
        scratch_shapes=[pltpu.VMEM((PEER_SLOTS * ROW_UNITS, LANES), F32)] * 4
        + [pltpu.VMEM((tb, PEER_SLOTS), F32)],
        compiler_params=_cp(("arbitrary",)),
        name="peer_act",
    )(_row_offset_pairs(eidx), tab_u, hc, g)


def _peer_out_kernel(idx_ref, tab_ref, c_ref, o_ref, cb0_ref, cb1_ref, *, tb):
    n_acc = 4

    def prepare(t, cb_ref):
        crow = c_ref[pl.ds(t, 1), :]
        cb_ref[...] = jnp.broadcast_to(crow, (PEER_SLOTS, PEER_SLOTS)).T

    def tok(t, cb_ref):
        idx_t = idx_ref.at[pl.ds(t * PEER_SLOTS, PEER_SLOTS)]
        low = lax.broadcasted_iota(I32, (SUBLANES, LANES), 0) < ROW_UNITS
        lo = [jnp.zeros((SUBLANES, LANES), F32) for _ in range(n_acc)]
        hi = [jnp.zeros((SUBLANES, LANES), F32) for _ in range(n_acc)]
        for p in range(PEER_SLOTS // 2):
            first, second = _load_pair(tab_ref, idx_t[2 * p], idx_t[2 * p + 1])
            cp = jnp.where(low,
                           jnp.broadcast_to(cb_ref[2 * p:2 * p + 1, :], (SUBLANES, LANES)),
                           jnp.broadcast_to(cb_ref[2 * p + 1:2 * p + 2, :], (SUBLANES, LANES)))
            lo[p % n_acc] = lo[p % n_acc] + cp * first
            hi[p % n_acc] = hi[p % n_acc] + cp * second
        lo = (lo[0] + lo[1]) + (lo[2] + lo[3])
        hi = (hi[0] + hi[1]) + (hi[2] + hi[3])
        o_ref[t] = jnp.concatenate([lo[0:ROW_UNITS, :] + lo[ROW_UNITS:, :],
                                    hi[0:ROW_UNITS, :] + hi[ROW_UNITS:, :]], axis=0)

    def pair(i, carry):
        prepare(2 * i + 1, cb1_ref)
        tok(2 * i, cb0_ref)
        prepare(jnp.minimum(2 * i + 2, tb - 1), cb0_ref)
        tok(2 * i + 1, cb1_ref)
        return carry

    prepare(0, cb0_ref)
    lax.fori_loop(0, tb // 2, pair, 0)


def _peer_out(eidx, coef, tab_v, tb=64):
    n = eidx.shape[0]
    chunks = 2 * ROW_UNITS
    out = pl.pallas_call(
        functools.partial(_peer_out_kernel, tb=tb),
        grid=(n // tb,),
        in_specs=[pl.BlockSpec((tb * PEER_SLOTS,), lambda i: (i,), memory_space=pltpu.SMEM),
                  pl.BlockSpec(memory_space=pltpu.VMEM),
                  pl.BlockSpec((tb, PEER_SLOTS), lambda i: (i, 0))],
        out_specs=pl.BlockSpec((tb, chunks, LANES), lambda i: (i, 0, 0)),
        out_shape=jax.ShapeDtypeStruct((n, chunks, LANES), F32),
        scratch_shapes=[pltpu.VMEM((PEER_SLOTS, PEER_SLOTS), F32)] * 2,
        compiler_params=_cp(("arbitrary",)),
        name="peer_out",
    )(_row_offsets(eidx), tab_v, coef)
    return out.reshape(n, chunks * LANES)


def _final_kernel(x1_ref, g2_ref, p_ref, y_ref):
    y_ref[...] = x1_ref[...] + g2_ref[...] * p_ref[...]


def _final(x1, gate2, peer, per_token, tm=512):
    bsz, t, d = x1.shape
    row = lambda b, i: (b, i, 0)
    mod_spec = (pl.BlockSpec((None, tm, d), row) if per_token
                else pl.BlockSpec((None, 1, d), lambda b, i: (b, 0, 0)))
    full = pl.BlockSpec((None, tm, d), row)
    return pl.pallas_call(
        _final_kernel,
        grid=(bsz, t // tm),
        in_specs=[full, mod_spec, full],
        out_specs=full,
        out_shape=jax.ShapeDtypeStruct((bsz, t, d), F32),
        compiler_params=_cp(("parallel", "parallel")),
        name="final",
    )(x1, gate2, peer)


def _prep_inproj(w_in, b_f, b_glu, q_norm_g, k_norm_g, tm):
    d = D_ATT
    nf = 3 * d + N_HEADS
    w = jnp.concatenate([w_in[:, :3 * d], w_in[:, nf:]], axis=1).astype(BF16)
    wf = jnp.pad(w_in[:, 3 * d:nf], ((0, 0), (0, LANES - N_HEADS)))
    wfh = wf.astype(BF16)
    wfl = (wf - wfh.astype(F32)).astype(BF16)
    bfp = jnp.pad(b_f, (0, LANES - N_HEADS)).reshape(1, LANES)
    qg = jnp.tile(q_norm_g, N_HEADS).reshape(1, d)
    kg = jnp.tile(k_norm_g, N_HEADS).reshape(1, d)
    hid = jnp.arange(d) // HEAD_DIM
    ones_bd = (hid[:, None] == hid[None, :]).astype(BF16)
    tri = jnp.tril(jnp.ones((tm, tm), BF16))
    return (w, wfh, wfl, bfp, b_glu.reshape(1, -1), qg, kg, ones_bd, tri)


def _peer_ffn(h2, wq_t, skh, skl, tab_u, tab_v):
    eidx, g = _route(h2, wq_t, skh, skl)
    coef = _peer_act(eidx, tab_u, h2, g)
    return _peer_out(eidx, coef, tab_v)


def kernel(x_prompt, x_sample, c_prompt, c_sample, cache_k, cache_v, cache_logf, state_conv,
           page_table, norm1_g, norm2_g, w_ada, b_ada, w_in, b_f, b_glu, q_norm_g, k_norm_g,
           conv_w, conv_b, conv_ln_g, conv_ln_b, w_o, peer_w_query, peer_sub_keys, peer_u,
           peer_v):
    depth = w_ada.shape[0]
    bsz, seq, d = x_prompt.shape
    n_dec, s_new, _ = x_sample.shape
    tm = 256
    xp, xs = x_prompt, x_sample.reshape(1, n_dec * s_new, d)
    outs = [[] for _ in range(8)]
    for l in range(depth):
        n_c = bsz + n_dec
        pad = (-n_c) % SUBLANES
        c_all = jnp.pad(jnp.concatenate([c_prompt, c_sample], axis=0), ((0, pad), (0, 0)))
        mod = _adaln(c_all, w_ada[l], b_ada[l])
        mod_p = [m.reshape(bsz, 1, d) for m in jnp.split(mod[:bsz], 6, axis=-1)]
        mod_s = [jnp.repeat(m, s_new, axis=0).reshape(1, n_dec * s_new, d)
                 for m in jnp.split(mod[bsz:n_c], 6, axis=-1)]

        wts = _prep_inproj(w_in[l], b_f[l], b_glu[l], q_norm_g[l], k_norm_g[l], tm)
        g1 = norm1_g[l].reshape(1, d)
        n2 = norm2_g[l].reshape(1, d)
        cw = conv_w[l]
        cb, lg, lb = (a[l].reshape(1, -1) for a in (conv_b, conv_ln_g, conv_ln_b))
        woa = w_o[l][:D_ATT].astype(BF16)
        woc = w_o[l][D_ATT:].astype(BF16)
        wq_t = peer_w_query[l].T.astype(BF16)
        sk = peer_sub_keys[l].reshape(PEER_HEADS * 2, PEER_N_KEYS, PEER_HALF)
        skh = sk.astype(BF16)
        skl = (sk - skh.astype(F32)).astype(BF16)
        tab_u = _pack_table(peer_u[l])
        tab_v = _pack_table(peer_v[l])

        sh1, sc1, gt1, sh2, sc2, gt2 = mod_p
        qb, k, v, kb, vb, lf, fc, u = _inproj(xp, sh1, sc1, g1, wts, per_token=False, tm=tm)
        a_out = _prompt_attention(qb, kb, vb, fc)
        c_out = _conv_prompt(u, cw, cb, lg, lb)
        x1, h2 = _tail(xp, a_out, c_out, gt1, sh2, sc2, n2, woa, woc, per_token=False, tm=tm)
        peer = _peer_ffn(h2.reshape(bsz * seq, d), wq_t, skh, skl, tab_u, tab_v)
        xp = _final(x1, gt2, peer.reshape(bsz, seq, d), per_token=False)
        outs[0].append(k.reshape(bsz, seq, N_HEADS, HEAD_DIM))
        outs[1].append(v.reshape(bsz, seq, N_HEADS, HEAD_DIM))
        outs[2].append(lf)
        outs[3].append(u[:, seq - (CONV_WIDTH - 1):, :])

        sh1, sc1, gt1, sh2, sc2, gt2 = mod_s
        qb, k, v, kb, vb, lf, fc, u = _inproj(xs, sh1, sc1, g1, wts, per_token=True, tm=tm)
        shp = (n_dec, s_new, D_ATT)
        a_out = _sample_attention(qb.reshape(shp).astype(F32), k.reshape(shp), v.reshape(shp),
                                  lf.reshape(n_dec, s_new, N_HEADS), cache_k[l], cache_v[l],
                                  cache_logf[l], page_table)
        u_s = u.reshape(shp)
        c_out = _conv_sample(state_conv[l].transpose(1, 0, 2), u_s.transpose(1, 0, 2),
                             cw, cb, lg, lb).transpose(1, 0, 2)
        x1, h2 = _tail(xs, a_out.reshape(1, -1, D_ATT), c_out.reshape(1, -1, D_ATT), gt1, sh2,
                       sc2, n2, woa, woc, per_token=True, tm=tm)
        peer = _peer_ffn(h2.reshape(n_dec * s_new, d), wq_t, skh, skl, tab_u, tab_v)
        xs = _final(x1, gt2, peer.reshape(1, n_dec * s_new, d), per_token=True)
        outs[4].append(k.reshape(n_dec, s_new, N_HEADS, HEAD_DIM))
        outs[5].append(v.reshape(n_dec, s_new, N_HEADS, HEAD_DIM))
        outs[6].append(lf.reshape(n_dec, s_new, N_HEADS))
        outs[7].append(jnp.concatenate([state_conv[l], u_s], axis=1)[:, -(CONV_WIDTH - 1):])
    st = [jnp.stack(o) for o in outs]
    return (xp, xs.reshape(n_dec, s_new, d), st[0], st[1], st[2], st[3], st[4], st[5], st[6],
            st[7])
```

```python
import functools

import jax
import jax.numpy as jnp
from jax import lax
from jax.experimental import pallas as pl
from jax.experimental.pallas import tpu as pltpu

F32 = jnp.float32
BF16 = jnp.bfloat16
I32 = jnp.int32

EPS = 1e-6
N_HEADS = 8
HEAD_DIM = 64
D_ATT = N_HEADS * HEAD_DIM
CONV_WIDTH = 31
PAGE_SIZE = 128
PEER_HEADS = 8
PEER_N_KEYS = 128
PEER_HALF = 128
PEER_TOPK = 16
PEER_SLOTS = PEER_HEADS * PEER_TOPK
ROW_UNITS = 4
LANES = 128
SUBLANES = 8
NEG = -1e30
VMEM_LIMIT = 56 * 1024 * 1024


def _cp(sem, vmem=VMEM_LIMIT):
    return pltpu.CompilerParams(dimension_semantics=sem, vmem_limit_bytes=vmem)


def _dot(a, b):
    return jnp.dot(a, b, preferred_element_type=F32)


def _dot_nt(a, b):
    return lax.dot_general(a, b, (((1,), (1,)), ((), ())), preferred_element_type=F32)


def _split2(a):
    hi = a.astype(BF16)
    lo = (a - hi.astype(F32)).astype(BF16)
    return hi, lo


def _split3(a):
    hi = a.astype(BF16)
    r = a - hi.astype(F32)
    mid = r.astype(BF16)
    lo = (r - mid.astype(F32)).astype(BF16)
    return hi, mid, lo


def _log_sigmoid(x):
    return jnp.minimum(x, 0.0) - jnp.log1p(jnp.exp(-jnp.abs(x)))


def _ada_kernel(c_ref, w_ref, b_ref, o_ref):
    ch, cl = _split2(c_ref[...])
    wh, wl = _split2(w_ref[...])
    o_ref[...] = _dot(ch, wh) + (_dot(ch, wl) + _dot(cl, wh)) + b_ref[...]


def _adaln(c, w_ada, b_ada):
    r, d = c.shape
    n = w_ada.shape[1]
    tn = 512
    return pl.pallas_call(
        _ada_kernel,
        grid=(n // tn,),
        in_specs=[pl.BlockSpec((r, d), lambda j: (0, 0)),
                  pl.BlockSpec((d, tn), lambda j: (0, j)),
                  pl.BlockSpec((1, tn), lambda j: (0, j))],
        out_specs=pl.BlockSpec((r, tn), lambda j: (0, j)),
        out_shape=jax.ShapeDtypeStruct((r, n), F32),
        compiler_params=_cp(("parallel",)),
        name="adaln",
    )(c, w_ada, b_ada.reshape(1, n))


def _inproj_kernel(x_ref, sh_ref, sc_ref, g1_ref, w_ref, wfh_ref, wfl_ref, bf_ref, bglu_ref,
                   qg_ref, kg_ref, ones_ref, tri_ref,
                   q_ref, k_ref, v_ref, kb_ref, vb_ref, lf_ref, fc_ref, u_ref, carry_ref):
    t = pl.program_id(1)

    @pl.when(t == 0)
    def _():
        carry_ref[...] = jnp.zeros_like(carry_ref)

    x = x_ref[...]
    ms = jnp.mean(x * x, axis=-1, keepdims=True)
    h = x * lax.rsqrt(ms + EPS) * g1_ref[...]
    h = h * (1.0 + sc_ref[...]) + sh_ref[...]
    hh, hl = _split2(h)
    z = _dot(hh, w_ref[...])
    fg = _dot(hh, wfh_ref[...]) + (_dot(hh, wfl_ref[...]) + _dot(hl, wfh_ref[...]))
    d = D_ATT
    q = z[:, 0:d]
    k = z[:, d:2 * d]
    v = z[:, 2 * d:3 * d]
    a = z[:, 3 * d:4 * d] + bglu_ref[:, 0:d]
    b = z[:, 4 * d:5 * d] + bglu_ref[:, d:2 * d]
    qs = _dot((q * q).astype(BF16), ones_ref[...])
    ks = _dot((k * k).astype(BF16), ones_ref[...])
    qn = q * lax.rsqrt(qs * (1.0 / HEAD_DIM) + EPS) * qg_ref[...]
    kn = k * lax.rsqrt(ks * (1.0 / HEAD_DIM) + EPS) * kg_ref[...]
    q_ref[...] = (qn * (HEAD_DIM ** -0.5)).astype(BF16)
    k_ref[...] = kn
    kb_ref[...] = kn.astype(BF16)
    v_ref[...] = v
    vb_ref[...] = v.astype(BF16)
    u_ref[...] = a * jax.nn.sigmoid(b)
    lf = _log_sigmoid(fg + bf_ref[...])
    lf_ref[...] = lf[:, 0:N_HEADS]
    l1, l2, l3 = _split3(lf)
    tri = tri_ref[...]
    fc = _dot(tri, l1) + (_dot(tri, l2) + _dot(tri, l3)) + carry_ref[...]
    fc_ref[...] = fc[:, 0:N_HEADS]
    tm = fc.shape[0]
    carry_ref[...] = fc[tm - 1:tm, :]


def _inproj(x, shift, scale, g1, wts, per_token, tm=256):
    bsz, t, d = x.shape
    w, wfh, wfl, bfp, bglu, qg, kg, ones_bd, tri = wts
    row = lambda b, i: (b, i, 0)
    const2 = lambda b, i: (0, 0)
    mod_spec = (pl.BlockSpec((None, tm, d), row) if per_token
                else pl.BlockSpec((None, 1, d), lambda b, i: (b, 0, 0)))
    full = lambda arr: pl.BlockSpec(arr.shape, const2)
    outs = [
        jax.ShapeDtypeStruct((bsz, t, D_ATT), BF16),
        jax.ShapeDtypeStruct((bsz, t, D_ATT), F32),
        jax.ShapeDtypeStruct((bsz, t, D_ATT), F32),
        jax.ShapeDtypeStruct((bsz, t, D_ATT), BF16),
        jax.ShapeDtypeStruct((bsz, t, D_ATT), BF16),
        jax.ShapeDtypeStruct((bsz, t, N_HEADS), F32),
        jax.ShapeDtypeStruct((bsz, t, N_HEADS), F32),
        jax.ShapeDtypeStruct((bsz, t, D_ATT), F32),
    ]
    ospec = lambda s: pl.BlockSpec((None, tm, s.shape[2]), row)
    return pl.pallas_call(
        _inproj_kernel,
        grid=(bsz, t // tm),
        in_specs=[pl.BlockSpec((None, tm, d), row), mod_spec, mod_spec, full(g1), full(w),
                  full(wfh), full(wfl), full(bfp), full(bglu), full(qg), full(kg),
                  full(ones_bd), full(tri)],
        out_specs=[ospec(s) for s in outs],
        out_shape=outs,
        scratch_shapes=[pltpu.VMEM((1, LANES), F32)],
        compiler_params=_cp(("parallel", "arbitrary")),
        name="inproj",
    )(x, shift, scale, g1, w, wfh, wfl, bfp, bglu, qg, kg, ones_bd, tri)


def _attn_kernel(q_ref, k_ref, v_ref, f_ref, o_ref, acc_ref, *, tq, tk):
    qi = pl.program_id(2)
    q = q_ref[...]
    lane = lax.broadcasted_iota(I32, (1, LANES), 1)
    zero = jnp.zeros_like(q)
    qm = [jnp.where(lane < HEAD_DIM, q, zero), jnp.where(lane >= HEAD_DIM, q, zero)]
    acc_ref[...] = jnp.zeros_like(acc_ref)
    rows = lax.broadcasted_iota(I32, (tq, tk), 0)
    cols = lax.broadcasted_iota(I32, (tq, tk), 1)

    def step(j, carry, diag_off=None):
        start = pl.multiple_of(j * tk, tk)
        kc = k_ref[pl.ds(start, tk), :]
        vc = v_ref[pl.ds(start, tk), :]
        fr = f_ref[j]
        new = []
        for hh in range(2):
            m, l = carry[2 * hh], carry[2 * hh + 1]
            s = _dot_nt(qm[hh], kc) - fr[hh:hh + 1, :]
            if diag_off is not None:
                s = jnp.where(cols + diag_off <= rows, s, NEG)
            m_new = jnp.maximum(m, jnp.max(s, axis=1, keepdims=True))
            alpha = jnp.exp(m - m_new)
            p = jnp.exp(s - m_new)
            l_new = alpha * l + jnp.sum(p, axis=1, keepdims=True)
            acc_ref[hh] = alpha * acc_ref[hh] + _dot(p.astype(BF16), vc)
            new += [m_new, l_new]
        return tuple(new)

    init = tuple(jnp.full((tq, 1), NEG, F32) if i % 2 == 0 else jnp.zeros((tq, 1), F32)
                 for i in range(4))
    n_diag = tq // tk
    carry = lax.fori_loop(0, qi * n_diag, step, init)
    for dd in range(n_diag):
        carry = step(qi * n_diag + dd, carry, diag_off=dd * tk)
    o0 = acc_ref[0] / carry[1]
    o1 = acc_ref[1] / carry[3]
    o_ref[...] = jnp.where(lane < HEAD_DIM, o0, o1).astype(o_ref.dtype)


def _prompt_attention(qb, kb, vb, fcum, tq=1024, tk=256):
    bsz, t, _ = qb.shape
    n_pairs = N_HEADS // 2
    nk = t // tk
    f_rows = fcum.transpose(0, 2, 1).reshape(bsz, n_pairs, 2, nk, tk).transpose(0, 1, 3, 2, 4)
    return pl.pallas_call(
        functools.partial(_attn_kernel, tq=tq, tk=tk),
        grid=(bsz, n_pairs, t // tq),
        in_specs=[pl.BlockSpec((None, tq, LANES), lambda b, p, i: (b, i, p)),
                  pl.BlockSpec((None, t, LANES), lambda b, p, i: (b, 0, p)),
                  pl.BlockSpec((None, t, LANES), lambda b, p, i: (b, 0, p)),
                  pl.BlockSpec((None, None, nk, 2, tk), lambda b, p, i: (b, p, 0, 0, 0))],
        out_specs=pl.BlockSpec((None, tq, LANES), lambda b, p, i: (b, i, p)),
        out_shape=jax.ShapeDtypeStruct((bsz, t, D_ATT), BF16),
        scratch_shapes=[pltpu.VMEM((2, tq, LANES), F32)],
        compiler_params=_cp(("parallel", "parallel", "arbitrary")),
        name="prompt_attn",
    )(qb, kb, vb, f_rows)


def _sattn_kernel(pt_ref, q_ref, *refs, n_new, n_pg):
    ck = refs[0:n_pg]
    cv = refs[n_pg:2 * n_pg]
    clf = refs[2 * n_pg:3 * n_pg]
    kn_ref, vn_ref, lfn_ref, triu_ref, o_ref, m_ref, l_ref, acc_ref, fcar_ref = refs[3 * n_pg:]
    p = pl.program_id(1)
    nq = n_new * N_HEADS
    lane = lax.broadcasted_iota(I32, (N_HEADS, D_ATT), 1)
    hrow = lax.broadcasted_iota(I32, (N_HEADS, D_ATT), 0)
    hmask = (lane // HEAD_DIM == hrow)

    @pl.when(p == 0)
    def _():
        m_ref[...] = jnp.full_like(m_ref, NEG)
        l_ref[...] = jnp.zeros_like(l_ref)
        acc_ref[...] = jnp.zeros_like(acc_ref)
        fcar_ref[...] = jnp.zeros_like(fcar_ref)

    q = q_ref[...]
    zq = jnp.zeros((N_HEADS, D_ATT), F32)
    qexp = jnp.concatenate(
        [jnp.where(hmask, jnp.broadcast_to(q[t:t + 1, :], (N_HEADS, D_ATT)), zq)
         for t in range(n_new)], axis=0).astype(BF16)

    def update(scores, values, transposed):
        m = m_ref[...]
        m_new = m
        for s in scores:
            m_new = jnp.maximum(m_new, jnp.max(s, axis=1, keepdims=True))
        alpha = jnp.exp(m - m_new)
        l_new = alpha * l_ref[...]
        acc = alpha * acc_ref[...]
        for s, vmat in zip(scores, values):
            pr = jnp.exp(s - m_new)
            l_new = l_new + jnp.sum(pr, axis=1, keepdims=True)
            pb = pr.astype(BF16)
            acc = acc + (_dot_nt(pb, vmat) if transposed else _dot(pb, vmat))
        l_ref[...] = l_new
        acc_ref[...] = acc
        m_ref[...] = m_new

    l1, l2, l3 = _split3(jnp.concatenate([r[...] for r in clf], axis=0))
    tri = triu_ref[...]
    local = _dot(l1, tri) + (_dot(l2, tri) + _dot(l3, tri))
    fcar = fcar_ref[...]
    scores = []
    for g in range(n_pg):
        fp = local[g * N_HEADS:(g + 1) * N_HEADS, :] + fcar
        fcar = jnp.broadcast_to(fp[:, PAGE_SIZE - 1:PAGE_SIZE], fcar.shape)
        scores.append(_dot(qexp, ck[g][...].astype(BF16))
                      - jnp.concatenate([fp] * n_new, axis=0))
    fcar_ref[...] = fcar
    update(scores, [cv[g][...].astype(BF16) for g in range(n_pg)], transposed=True)

    @pl.when(p == pl.num_programs(1) - 1)
    def _():
        lfn = lfn_ref[...]
        cols = []
        run = fcar[:, 0:1]
        for t in range(n_new):
            run = run + lfn[:, t:t + 1]
            cols.append(run)
        pad = SUBLANES - n_new
        fnew = jnp.concatenate(cols + [jnp.zeros((N_HEADS, pad), F32)], axis=1)
        zk = jnp.zeros((pad, D_ATT), F32)
        kn = jnp.concatenate([kn_ref[...], zk], axis=0).astype(BF16)
        vn = jnp.concatenate([vn_ref[...], zk], axis=0).astype(BF16)
        s2 = _dot_nt(qexp, kn) - jnp.concatenate([fnew] * n_new, axis=0)
        r = lax.broadcasted_iota(I32, (nq, SUBLANES), 0) // N_HEADS
        c = lax.broadcasted_iota(I32, (nq, SUBLANES), 1)
        update([jnp.where(c <= r, s2, NEG)], [vn], transposed=False)
        out = acc_ref[...] / l_ref[...]
        zo = jnp.zeros((N_HEADS, D_ATT), F32)
        rows = [jnp.sum(jnp.where(hmask, out[t * N_HEADS:(t + 1) * N_HEADS, :], zo),
                        axis=0, keepdims=True) for t in range(n_new)]
        o_ref[...] = jnp.concatenate(rows, axis=0)


def _sample_attention(qb, kn, vn, lfn, cache_k, cache_v, cache_logf, page_table, n_pg=8):
    n, s_new, _ = qb.shape
    n_pages = page_table.shape[1]
    n_phys = cache_k.shape[0]
    nq = s_new * N_HEADS
    ck = cache_k.transpose(0, 2, 3, 1).reshape(n_phys, D_ATT, PAGE_SIZE)
    cv = cache_v.transpose(0, 2, 3, 1).reshape(n_phys, D_ATT, PAGE_SIZE)
    clf = cache_logf.transpose(0, 2, 1)
    lfn_t = lfn.transpose(0, 2, 1)
    triu = jnp.triu(jnp.ones((PAGE_SIZE, PAGE_SIZE), BF16))
    seq = lambda i, p, pt: (i, 0, 0)
    tok_spec = pl.BlockSpec((None, s_new, D_ATT), seq)

    def page(g):
        return lambda i, p, pt: (pt[i * n_pages + p * n_pg + g], 0, 0)

    kv_specs = [pl.BlockSpec((None, D_ATT, PAGE_SIZE), page(g)) for g in range(n_pg)]
    lf_specs = [pl.BlockSpec((None, N_HEADS, PAGE_SIZE), page(g)) for g in range(n_pg)]
    gs = pltpu.PrefetchScalarGridSpec(
        num_scalar_prefetch=1,
        grid=(n, n_pages // n_pg),
        in_specs=[tok_spec] + kv_specs + kv_specs + lf_specs + [
            tok_spec, tok_spec,
            pl.BlockSpec((None, N_HEADS, s_new), seq),
            pl.BlockSpec(triu.shape, lambda i, p, pt: (0, 0))],
        out_specs=tok_spec,
        scratch_shapes=[pltpu.VMEM((nq, 1), F32), pltpu.VMEM((nq, 1), F32),
                        pltpu.VMEM((nq, D_ATT), F32),
                        pltpu.VMEM((N_HEADS, PAGE_SIZE), F32)],
    )
    return pl.pallas_call(
        functools.partial(_sattn_kernel, n_new=s_new, n_pg=n_pg),
        grid_spec=gs,
        out_shape=jax.ShapeDtypeStruct((n, s_new, D_ATT), F32),
        compiler_params=_cp(("parallel", "arbitrary")),
        name="sample_attn",
    )(page_table.reshape(-1), qb, *([ck] * n_pg), *([cv] * n_pg), *([clf] * n_pg),
      kn, vn, lfn_t, triu)


def _ln_swish(y, g, b):
    mu = jnp.mean(y, axis=-1, keepdims=True)
    yc = y - mu
    var = jnp.mean(yc * yc, axis=-1, keepdims=True)
    yn = yc * lax.rsqrt(var + EPS) * g + b
    return yn * jax.nn.sigmoid(yn)


def _conv_kernel(u_ref, cw_ref, cb_ref, g_ref, b_ref, o_ref, ext_ref, *, tm, hist):
    t = pl.program_id(1)

    @pl.when(t == 0)
    def _():
        ext_ref[0:hist, :] = jnp.zeros((hist, ext_ref.shape[1]), F32)

    ext_ref[hist:hist + tm, :] = u_ref[...]
    off = hist - (CONV_WIDTH - 1)
    y = jnp.zeros(u_ref.shape, F32) + cb_ref[...]
    for w in range(CONV_WIDTH):
        y = y + ext_ref[off + w:off + w + tm, :] * cw_ref[w:w + 1, :]
    o_ref[...] = _ln_swish(y, g_ref[...], b_ref[...]).astype(o_ref.dtype)
    ext_ref[0:hist, :] = ext_ref[tm:tm + hist, :]


def _conv_prompt(u, cw, cb, g, b, tm=512):
    bsz, t, c = u.shape
    hist = 32
    row = lambda bb, i: (bb, i, 0)
    const = lambda bb, i: (0, 0)
    return pl.pallas_call(
        functools.partial(_conv_kernel, tm=tm, hist=hist),
        grid=(bsz, t // tm),
        in_specs=[pl.BlockSpec((None, tm, c), row), pl.BlockSpec(cw.shape, const),
                  pl.BlockSpec((1, c), const), pl.BlockSpec((1, c), const),
                  pl.BlockSpec((1, c), const)],
        out_specs=pl.BlockSpec((None, tm, c), row),
        out_shape=jax.ShapeDtypeStruct((bsz, t, c), BF16),
        scratch_shapes=[pltpu.VMEM((hist + tm, c), F32)],
        compiler_params=_cp(("parallel", "arbitrary")),
        name="conv_prompt",
    )(u, cw, cb, g, b)


def _conv_step_kernel(st_ref, u_ref, cw_ref, cb_ref, g_ref, b_ref, o_ref, *, n_hist, n_new):
    def ext(i):
        return st_ref[i] if i < n_hist else u_ref[i - n_hist]

    for t in range(n_new):
        y = jnp.zeros(o_ref.shape[1:], F32) + cb_ref[...]
        for w in range(CONV_WIDTH):
            y = y + ext(t + w) * cw_ref[w:w + 1, :]
        o_ref[t] = _ln_swish(y, g_ref[...], b_ref[...]).astype(o_ref.dtype)


def _conv_sample(state_t, u_t, cw, cb, g, b):
    n_hist, n, c = state_t.shape
    n_new = u_t.shape[0]
    return pl.pallas_call(
        functools.partial(_conv_step_kernel, n_hist=n_hist, n_new=n_new),
        out_shape=jax.ShapeDtypeStruct((n_new, n, c), BF16),
        compiler_params=pltpu.CompilerParams(vmem_limit_bytes=VMEM_LIMIT),
        name="conv_sample",
    )(state_t, u_t, cw, cb, g, b)


def _tail_kernel(x_ref, a_ref, c_ref, g1_ref, sh_ref, sc_ref, n2_ref, woa_ref, woc_ref,
                 x1_ref, h2_ref):
    proj = (_dot(a_ref[...].astype(BF16), woa_ref[...])
            + _dot(c_ref[...].astype(BF16), woc_ref[...]))
    x1 = x_ref[...] + g1_ref[...] * proj
    x1_ref[...] = x1
    ms = jnp.mean(x1 * x1, axis=-1, keepdims=True)
    h2 = x1 * lax.rsqrt(ms + EPS) * n2_ref[...]
    h2_ref[...] = h2 * (1.0 + sc_ref[...]) + sh_ref[...]


def _tail(x, attn, conv, gate1, shift2, scale2, n2g, woa, woc, per_token, tm=256):
    bsz, t, d = x.shape
    row = lambda b, i: (b, i, 0)
    const2 = lambda b, i: (0, 0)
    mod_spec = (pl.BlockSpec((None, tm, d), row) if per_token
                else pl.BlockSpec((None, 1, d), lambda b, i: (b, 0, 0)))
    half = pl.BlockSpec((None, tm, D_ATT), row)
    full = pl.BlockSpec((None, tm, d), row)
    return pl.pallas_call(
        _tail_kernel,
        grid=(bsz, t // tm),
        in_specs=[full, half, half, mod_spec, mod_spec, mod_spec,
                  pl.BlockSpec(n2g.shape, const2), pl.BlockSpec(woa.shape, const2),
                  pl.BlockSpec(woc.shape, const2)],
        out_specs=[full, full],
        out_shape=[jax.ShapeDtypeStruct((bsz, t, d), F32)] * 2,
        compiler_params=_cp(("parallel", "parallel")),
        name="tail",
    )(x, attn, conv, gate1, shift2, scale2, n2g, woa, woc)


def _top_rows(s, order, payload, k):
    big = float(2 ** 24)
    vals, outs = [], []
    for _ in range(k):
        m = jnp.max(s, axis=0, keepdims=True)
        pos = jnp.min(jnp.where(s == m, order, big), axis=0, keepdims=True)
        hit = order == pos
        vals.append(m)
        if payload is None:
            outs.append(pos)
        else:
            outs.append(jnp.sum(jnp.where(hit, payload, 0), axis=0, keepdims=True))
        s = jnp.where(hit, -jnp.inf, s)
    return jnp.concatenate(vals, axis=0), jnp.concatenate(outs, axis=0)


def _pair_groups():
    groups = [(0, PEER_TOPK, PEER_TOPK)]
    for a in range(1, SUBLANES):
        groups.append((a, SUBLANES, PEER_TOPK // (a + 1)))
    groups.append((None, SUBLANES, SUBLANES))
    return groups


def _route_kernel(h_ref, wq_ref, skh_ref, skl_ref, e_ref, g_ref):
    hb = h_ref[...].astype(BF16)
    tb = hb.shape[0]
    key_order = lax.broadcasted_iota(I32, (PEER_N_KEYS, tb), 0).astype(F32)
    groups = _pair_groups()
    orders, masks = [], []
    for a, rows, valid in groups:
        r = lax.broadcasted_iota(I32, (rows, tb), 0)
        flat = (a * PEER_TOPK + r) if a is not None else (SUBLANES + r) * PEER_TOPK
        orders.append(flat.astype(F32))
        masks.append(r < valid)
    pair_order = jnp.concatenate(orders, axis=0)
    pair_valid = jnp.concatenate(masks, axis=0)
    e_rows, g_rows = [], []
    for hd in range(PEER_HEADS):
        tops = []
        for half in range(2):
            i = hd * 2 + half
            wq = wq_ref[i * PEER_HALF:(i + 1) * PEER_HALF, :]
            qt = _dot_nt(wq, hb)
            qh, ql = _split2(qt)
            skh = skh_ref[i]
            s = _dot(skh, qh) + (_dot(skh, ql) + _dot(skl_ref[i], qh))
            vals, pos = _top_rows(s, key_order, None, PEER_TOPK)
            tops.append((vals, pos.astype(I32)))
        (s1, i1), (s2, i2) = tops
        cand, cidx = [], []
        for a, rows, _ in groups:
            if a is not None:
                cand.append(s1[a:a + 1, :] + s2[0:rows, :])
                cidx.append(i1[a:a + 1, :] * PEER_N_KEYS + i2[0:rows, :])
            else:
                cand.append(s1[SUBLANES:, :] + s2[0:1, :])
                cidx.append(i1[SUBLANES:, :] * PEER_N_KEYS + i2[0:1, :])
        cand = jnp.where(pair_valid, jnp.concatenate(cand, axis=0), -jnp.inf)
        top, eidx = _top_rows(cand, pair_order, jnp.concatenate(cidx, axis=0), PEER_TOPK)
        ex = jnp.exp(top - top[0:1, :])
        g_rows.append(ex / jnp.sum(ex, axis=0, keepdims=True))
        e_rows.append(eidx)
    e_ref[...] = jnp.concatenate(e_rows, axis=0).T
    g_ref[...] = jnp.concatenate(g_rows, axis=0).T


def _route(h2, wq_t, skh, skl, tb=128):
    n, d = h2.shape
    const2 = lambda i: (0, 0)
    const3 = lambda i: (0, 0, 0)
    return pl.pallas_call(
        _route_kernel,
        grid=(n // tb,),
        in_specs=[pl.BlockSpec((tb, d), lambda i: (i, 0)), pl.BlockSpec(wq_t.shape, const2),
                  pl.BlockSpec(skh.shape, const3), pl.BlockSpec(skl.shape, const3)],
        out_specs=[pl.BlockSpec((tb, PEER_SLOTS), lambda i: (i, 0))] * 2,
        out_shape=[jax.ShapeDtypeStruct((n, PEER_SLOTS), I32),
                   jax.ShapeDtypeStruct((n, PEER_SLOTS), F32)],
        compiler_params=_cp(("parallel",)),
        name="peer_route",
    )(h2, wq_t, skh, skl)


def _pack_table(tab):
    e, d = tab.shape
    bits = lax.bitcast_convert_type(tab.astype(BF16), jnp.uint16).astype(jnp.uint32)
    words = bits[:, :d // 2] | (bits[:, d // 2:] << 16)
    return jnp.pad(words.reshape(e * ROW_UNITS, LANES), ((ROW_UNITS, ROW_UNITS), (0, 0)))


def _row_offsets(eidx):
    return (eidx * ROW_UNITS).reshape(-1)


def _row_offset_pairs(eidx):
    n, slots = eidx.shape
    off = (eidx * ROW_UNITS).astype(jnp.uint16)
    pairs = lax.bitcast_convert_type(off.reshape(n, slots // 2, 2), jnp.uint32)
    return lax.bitcast_convert_type(pairs, I32).reshape(-1)


def _load_pair(tab_ref, off_a, off_b):
    low = lax.broadcasted_iota(I32, (SUBLANES, LANES), 0) < ROW_UNITS
    w = jnp.where(low, tab_ref[pl.ds(off_a + ROW_UNITS, SUBLANES), :],
                  tab_ref[pl.ds(off_b, SUBLANES), :])
    first = lax.bitcast_convert_type(w << 16, F32)
    second = lax.bitcast_convert_type(w & jnp.uint32(0xFFFF0000), F32)
    return first, second


def _peer_act_kernel(idx_ref, tab_ref, h_ref, g_ref, c_ref, ps0_ref, ps1_ref, ps2_ref, ps3_ref,
                     act_ref, *, tb):
    def gather(t, ps_ref):
        hv = h_ref[t]
        h_first = jnp.concatenate([hv[0:ROW_UNITS, :]] * 2, axis=0)
        h_second = jnp.concatenate([hv[ROW_UNITS:, :]] * 2, axis=0)
        idx_t = idx_ref.at[pl.ds(t * (PEER_SLOTS // 2), PEER_SLOTS // 2)]
        for p in range(PEER_SLOTS // 2):
            packed = idx_t[p]
            first, second = _load_pair(tab_ref, packed & 0xFFFF,
                                       lax.shift_right_logical(packed, 16))
            ps_ref[p * SUBLANES:(p + 1) * SUBLANES, :] = first * h_first + second * h_second

    def reduce(t, ps_ref):
        s = ps_ref[pl.ds(0, PEER_SLOTS, stride=ROW_UNITS), :]
        for r in range(1, ROW_UNITS):
            s = s + ps_ref[pl.ds(r, PEER_SLOTS, stride=ROW_UNITS), :]
        act_ref[pl.ds(t, 1), :] = jnp.sum(s.T, axis=0, keepdims=True)

    ps = (ps0_ref, ps1_ref, ps2_ref, ps3_ref)

    per_trip = 4

    def trip(i, carry):
        for k in range(per_trip):
            gather(per_trip * i + k, ps[k % 4])
            reduce(jnp.maximum(per_trip * i + k - 2, 0), ps[(k + 2) % 4])
        return carry

    ps2_ref[...] = jnp.zeros_like(ps2_ref)
    ps3_ref[...] = jnp.zeros_like(ps3_ref)
    lax.fori_loop(0, tb // per_trip, trip, 0)
    reduce(tb - 2, ps2_ref)
    reduce(tb - 1, ps3_ref)
    act = act_ref[...]
    gelu = 0.5 * act * (1.0 + lax.erf(act * (2.0 ** -0.5)))
    c_ref[...] = g_ref[...] * gelu


def _peer_act(eidx, tab_u, h2, g, tb=64):
    n = eidx.shape[0]
    d = h2.shape[1]
    hc = h2.reshape(n, d // LANES, LANES)
    return pl.pallas_call(
        functools.partial(_peer_act_kernel, tb=tb),
        grid=(n // tb,),
        in_specs=[pl.BlockSpec((tb * PEER_SLOTS // 2,), lambda i: (i,), memory_space=pltpu.SMEM),
                  pl.BlockSpec(memory_space=pltpu.VMEM),
                  pl.BlockSpec((tb, d // LANES, LANES), lambda i: (i, 0, 0)),
                  pl.BlockSpec((tb, PEER_SLOTS), lambda i: (i, 0))],
        out_specs=pl.BlockSpec((tb, PEER_SLOTS), lambda i: (i, 0)),
        out_shape=jax.ShapeDtypeStruct((n, PEER_SLOTS), F32),
        scratch_shapes=[pltpu.VMEM((PEER_SLOTS * ROW_UNITS, LANES), F32)] * 4
        + [pltpu.VMEM((tb, PEER_SLOTS), F32)],
        compiler_params=_cp(("arbitrary",)),
        name="peer_act",
    )(_row_offset_pairs(eidx), tab_u, hc, g)


def _peer_out_kernel(idx_ref, tab_ref, c_ref, o_ref, cb0_ref, cb1_ref, *, tb):
    n_acc = 4

    def prepare(t, cb_ref):
        crow = c_ref[pl.ds(t, 1), :]
        cb_ref[...] = jnp.broadcast_to(crow, (PEER_SLOTS, PEER_SLOTS)).T

    def tok(t, cb_ref):
        idx_t = idx_ref.at[pl.ds(t * PEER_SLOTS, PEER_SLOTS)]
        low = lax.broadcasted_iota(I32, (SUBLANES, LANES), 0) < ROW_UNITS
        lo = [jnp.zeros((SUBLANES, LANES), F32) for _ in range(n_acc)]
        hi = [jnp.zeros((SUBLANES, LANES), F32) for _ in range(n_acc)]
        for p in range(PEER_SLOTS // 2):
            first, second = _load_pair(tab_ref, idx_t[2 * p], idx_t[2 * p + 1])
            cp = jnp.where(low,
                           jnp.broadcast_to(cb_ref[2 * p:2 * p + 1, :], (SUBLANES, LANES)),
                           jnp.broadcast_to(cb_ref[2 * p + 1:2 * p + 2, :], (SUBLANES, LANES)))
            lo[p % n_acc] = lo[p % n_acc] + cp * first
            hi[p % n_acc] = hi[p % n_acc] + cp * second
        lo = (lo[0] + lo[1]) + (lo[2] + lo[3])
        hi = (hi[0] + hi[1]) + (hi[2] + hi[3])
        o_ref[t] = jnp.concatenate([lo[0:ROW_UNITS, :] + lo[ROW_UNITS:, :],
                                    hi[0:ROW_UNITS, :] + hi[ROW_UNITS:, :]], axis=0)

    def pair(i, carry):
        prepare(2 * i + 1, cb1_ref)
        tok(2 * i, cb0_ref)
        prepare(jnp.minimum(2 * i + 2, tb - 1), cb0_ref)
        tok(2 * i + 1, cb1_ref)
        return carry

    prepare(0, cb0_ref)
    lax.fori_loop(0, tb // 2, pair, 0)


def _peer_out(eidx, coef, tab_v, tb=64):
    n = eidx.shape[0]
    chunks = 2 * ROW_UNITS
    out = pl.pallas_call(
        functools.partial(_peer_out_kernel, tb=tb),
        grid=(n // tb,),
        in_specs=[pl.BlockSpec((tb * PEER_SLOTS,), lambda i: (i,), memory_space=pltpu.SMEM),
                  pl.BlockSpec(memory_space=pltpu.VMEM),
                  pl.BlockSpec((tb, PEER_SLOTS), lambda i: (i, 0))],
        out_specs=pl.BlockSpec((tb, chunks, LANES), lambda i: (i, 0, 0)),
        out_shape=jax.ShapeDtypeStruct((n, chunks, LANES), F32),
        scratch_shapes=[pltpu.VMEM((PEER_SLOTS, PEER_SLOTS), F32)] * 2,
        compiler_params=_cp(("arbitrary",)),
        name="peer_out",
    )(_row_offsets(eidx), tab_v, coef)
    return out.reshape(n, chunks * LANES)


def _final_kernel(x1_ref, g2_ref, p_ref, y_ref):
    y_ref[...] = x1_ref[...] + g2_ref[...] * p_ref[...]


def _final(x1, gate2, peer, per_token, tm=512):
    bsz, t, d = x1.shape
    row = lambda b, i: (b, i, 0)
    mod_spec = (pl.BlockSpec((None, tm, d), row) if per_token
                else pl.BlockSpec((None, 1, d), lambda b, i: (b, 0, 0)))
    full = pl.BlockSpec((None, tm, d), row)
    return pl.pallas_call(
        _final_kernel,
        grid=(bsz, t // tm),
        in_specs=[full, mod_spec, full],
        out_specs=full,
        out_shape=jax.ShapeDtypeStruct((bsz, t, d), F32),
        compiler_params=_cp(("parallel", "parallel")),
        name="final",
    )(x1, gate2, peer)


def _prep_inproj(w_in, b_f, b_glu, q_norm_g, k_norm_g, tm):
    d = D_ATT
    nf = 3 * d + N_HEADS
    w = jnp.concatenate([w_in[:, :3 * d], w_in[:, nf:]], axis=1).astype(BF16)
    wf = jnp.pad(w_in[:, 3 * d:nf], ((0, 0), (0, LANES - N_HEADS)))
    wfh = wf.astype(BF16)
    wfl = (wf - wfh.astype(F32)).astype(BF16)
    bfp = jnp.pad(b_f, (0, LANES - N_HEADS)).reshape(1, LANES)
    qg = jnp.tile(q_norm_g, N_HEADS).reshape(1, d)
    kg = jnp.tile(k_norm_g, N_HEADS).reshape(1, d)
    hid = jnp.arange(d) // HEAD_DIM
    ones_bd = (hid[:, None] == hid[None, :]).astype(BF16)
    tri = jnp.tril(jnp.ones((tm, tm), BF16))
    return (w, wfh, wfl, bfp, b_glu.reshape(1, -1), qg, kg, ones_bd, tri)


def _peer_ffn(h2, wq_t, skh, skl, tab_u, tab_v):
    eidx, g = _route(h2, wq_t, skh, skl)
    coef = _peer_act(eidx, tab_u, h2, g)
    return _peer_out(eidx, coef, tab_v)


def kernel(x_prompt, x_sample, c_prompt, c_sample, cache_k, cache_v, cache_logf, state_conv,
           page_table, norm1_g, norm2_g, w_ada, b_ada, w_in, b_f, b_glu, q_norm_g, k_norm_g,
           conv_w, conv_b, conv_ln_g, conv_ln_b, w_o, peer_w_query, peer_sub_keys, peer_u,
           peer_v):
    depth = w_ada.shape[0]
    bsz, seq, d = x_prompt.shape
    n_dec, s_new, _ = x_sample.shape
    tm = 256
    xp, xs = x_prompt, x_sample.reshape(1, n_dec * s_new, d)
    outs = [[] for _ in range(8)]
    for l in range(depth):
        n_c = bsz + n_dec
        pad = (-n_c) % SUBLANES
        c_all = jnp.pad(jnp.concatenate([c_prompt, c_sample], axis=0), ((0, pad), (0, 0)))
        mod = _adaln(c_all, w_ada[l], b_ada[l])
        mod_p = [m.reshape(bsz, 1, d) for m in jnp.split(mod[:bsz], 6, axis=-1)]
        mod_s = [jnp.repeat(m, s_new, axis=0).reshape(1, n_dec * s_new, d)
                 for m in jnp.split(mod[bsz:n_c], 6, axis=-1)]

        wts = _prep_inproj(w_in[l], b_f[l], b_glu[l], q_norm_g[l], k_norm_g[l], tm)
        g1 = norm1_g[l].reshape(1, d)
        n2 = norm2_g[l].reshape(1, d)
        cw = conv_w[l]
        cb, lg, lb = (a[l].reshape(1, -1) for a in (conv_b, conv_ln_g, conv_ln_b))
        woa = w_o[l][:D_ATT].astype(BF16)
        woc = w_o[l][D_ATT:].astype(BF16)
        wq_t = peer_w_query[l].T.astype(BF16)
        sk = peer_sub_keys[l].reshape(PEER_HEADS * 2, PEER_N_KEYS, PEER_HALF)
        skh = sk.astype(BF16)
        skl = (sk - skh.astype(F32)).astype(BF16)
        tab_u = _pack_table(peer_u[l])
        tab_v = _pack_table(peer_v[l])

        sh1, sc1, gt1, sh2, sc2, gt2 = mod_p
        qb, k, v, kb, vb, lf, fc, u = _inproj(xp, sh1, sc1, g1, wts, per_token=False, tm=tm)
        a_out = _prompt_attention(qb, kb, vb, fc)
        c_out = _conv_prompt(u, cw, cb, lg, lb)
        x1, h2 = _tail(xp, a_out, c_out, gt1, sh2, sc2, n2, woa, woc, per_token=False, tm=tm)
        peer = _peer_ffn(h2.reshape(bsz * seq, d), wq_t, skh, skl, tab_u, tab_v)
        xp = _final(x1, gt2, peer.reshape(bsz, seq, d), per_token=False)
        outs[0].append(k.reshape(bsz, seq, N_HEADS, HEAD_DIM))
        outs[1].append(v.reshape(bsz, seq, N_HEADS, HEAD_DIM))
        outs[2].append(lf)
        outs[3].append(u[:, seq - (CONV_WIDTH - 1):, :])

        sh1, sc1, gt1, sh2, sc2, gt2 = mod_s
        qb, k, v, kb, vb, lf, fc, u = _inproj(xs, sh1, sc1, g1, wts, per_token=True, tm=tm)
        shp = (n_dec, s_new, D_ATT)
        a_out = _sample_attention(qb.reshape(shp).astype(F32), k.reshape(shp), v.reshape(shp),
                                  lf.reshape(n_dec, s_new, N_HEADS), cache_k[l], cache_v[l],
                                  cache_logf[l], page_table)
        u_s = u.reshape(shp)
        c_out = _conv_sample(state_conv[l].transpose(1, 0, 2), u_s.transpose(1, 0, 2),
                             cw, cb, lg, lb).transpose(1, 0, 2)
        x1, h2 = _tail(xs, a_out.reshape(1, -1, D_ATT), c_out.reshape(1, -1, D_ATT), gt1, sh2,
                       sc2, n2, woa, woc, per_token=True, tm=tm)
        peer = _peer_ffn(h2.reshape(n_dec * s_new, d), wq_t, skh, skl, tab_u, tab_v)
        xs = _final(x1, gt2, peer.reshape(1, n_dec * s_new, d), per_token=True)
        outs[4].append(k.reshape(n_dec, s_new, N_HEADS, HEAD_DIM))
        outs[5].append(v.reshape(n_dec, s_new, N_HEADS, HEAD_DIM))
        outs[6].append(lf.reshape(n_dec, s_new, N_HEADS))
        outs[7].append(jnp.concatenate([state_conv[l], u_s], axis=1)[:, -(CONV_WIDTH - 1):])
    st = [jnp.stack(o) for o in outs]
    return (xp, xs.reshape(n_dec, s_new, d), st[0], st[1], st[2], st[3], st[4], st[5], st[6],
            st[7])
```

```python
import functools

import jax
import jax.numpy as jnp
from jax import lax
from jax.experimental import pallas as pl
from jax.experimental.pallas import tpu as pltpu

F32 = jnp.float32
BF16 = jnp.bfloat16
I32 = jnp.int32

EPS = 1e-6
N_HEADS = 8
HEAD_DIM = 64
D_ATT = N_HEADS * HEAD_DIM
CONV_WIDTH = 31
PAGE_SIZE = 128
PEER_HEADS = 8
PEER_N_KEYS = 128
PEER_HALF = 128
PEER_TOPK = 16
PEER_SLOTS = PEER_HEADS * PEER_TOPK
LANES = 128
SUBLANES = 8
NEG = -1e30
VMEM_LIMIT = 56 * 1024 * 1024


def _cp(sem, vmem=VMEM_LIMIT):
    return pltpu.CompilerParams(dimension_semantics=sem, vmem_limit_bytes=vmem)


def _dot(a, b):
    return jnp.dot(a, b, preferred_element_type=F32)


def _dot_nt(a, b):
    return lax.dot_general(a, b, (((1,), (1,)), ((), ())), preferred_element_type=F32)


def _split2(a):
    hi = a.astype(BF16)
    lo = (a - hi.astype(F32)).astype(BF16)
    return hi, lo


def _split3(a):
    hi = a.astype(BF16)
    r = a - hi.astype(F32)
    mid = r.astype(BF16)
    lo = (r - mid.astype(F32)).astype(BF16)
    return hi, mid, lo


def _log_sigmoid(x):
    return jnp.minimum(x, 0.0) - jnp.log1p(jnp.exp(-jnp.abs(x)))


def _ada_kernel(c_ref, w_ref, b_ref, o_ref):
    ch, cl = _split2(c_ref[...])
    wh, wl = _split2(w_ref[...])
    o_ref[...] = _dot(ch, wh) + (_dot(ch, wl) + _dot(cl, wh)) + b_ref[...]


def _adaln(c, w_ada, b_ada):
    r, d = c.shape
    n = w_ada.shape[1]
    tn = 512
    return pl.pallas_call(
        _ada_kernel,
        grid=(n // tn,),
        in_specs=[pl.BlockSpec((r, d), lambda j: (0, 0)),
                  pl.BlockSpec((d, tn), lambda j: (0, j)),
                  pl.BlockSpec((1, tn), lambda j: (0, j))],
        out_specs=pl.BlockSpec((r, tn), lambda j: (0, j)),
        out_shape=jax.ShapeDtypeStruct((r, n), F32),
        compiler_params=_cp(("parallel",)),
        name="adaln",
    )(c, w_ada, b_ada.reshape(1, n))


def _inproj_kernel(x_ref, sh_ref, sc_ref, g1_ref, w_ref, wfh_ref, wfl_ref, bf_ref, bglu_ref,
                   qg_ref, kg_ref, ones_ref, tri_ref,
                   q_ref, k_ref, v_ref, kb_ref, vb_ref, lf_ref, fc_ref, u_ref, carry_ref):
    t = pl.program_id(1)

    @pl.when(t == 0)
    def _():
        carry_ref[...] = jnp.zeros_like(carry_ref)

    x = x_ref[...]
    ms = jnp.mean(x * x, axis=-1, keepdims=True)
    h = x * lax.rsqrt(ms + EPS) * g1_ref[...]
    h = h * (1.0 + sc_ref[...]) + sh_ref[...]
    hh, hl = _split2(h)
    z = _dot(hh, w_ref[...])
    fg = _dot(hh, wfh_ref[...]) + (_dot(hh, wfl_ref[...]) + _dot(hl, wfh_ref[...]))
    d = D_ATT
    q = z[:, 0:d]
    k = z[:, d:2 * d]
    v = z[:, 2 * d:3 * d]
    a = z[:, 3 * d:4 * d] + bglu_ref[:, 0:d]
    b = z[:, 4 * d:5 * d] + bglu_ref[:, d:2 * d]
    qs = _dot((q * q).astype(BF16), ones_ref[...])
    ks = _dot((k * k).astype(BF16), ones_ref[...])
    qn = q * lax.rsqrt(qs * (1.0 / HEAD_DIM) + EPS) * qg_ref[...]
    kn = k * lax.rsqrt(ks * (1.0 / HEAD_DIM) + EPS) * kg_ref[...]
    q_ref[...] = (qn * (HEAD_DIM ** -0.5)).astype(BF16)
    k_ref[...] = kn
    kb_ref[...] = kn.astype(BF16)
    v_ref[...] = v
    vb_ref[...] = v.astype(BF16)
    u_ref[...] = a * jax.nn.sigmoid(b)
    lf = _log_sigmoid(fg + bf_ref[...])
    lf_ref[...] = lf[:, 0:N_HEADS]
    l1, l2, l3 = _split3(lf)
    tri = tri_ref[...]
    fc = _dot(tri, l1) + (_dot(tri, l2) + _dot(tri, l3)) + carry_ref[...]
    fc_ref[...] = fc[:, 0:N_HEADS]
    tm = fc.shape[0]
    carry_ref[...] = fc[tm - 1:tm, :]


def _inproj(x, shift, scale, g1, wts, per_token, tm=256):
    bsz, t, d = x.shape
    w, wfh, wfl, bfp, bglu, qg, kg, ones_bd, tri = wts
    row = lambda b, i: (b, i, 0)
    const2 = lambda b, i: (0, 0)
    mod_spec = (pl.BlockSpec((None, tm, d), row) if per_token
                else pl.BlockSpec((None, 1, d), lambda b, i: (b, 0, 0)))
    full = lambda arr: pl.BlockSpec(arr.shape, const2)
    outs = [
        jax.ShapeDtypeStruct((bsz, t, D_ATT), BF16),
        jax.ShapeDtypeStruct((bsz, t, D_ATT), F32),
        jax.ShapeDtypeStruct((bsz, t, D_ATT), F32),
        jax.ShapeDtypeStruct((bsz, t, D_ATT), BF16),
        jax.ShapeDtypeStruct((bsz, t, D_ATT), BF16),
        jax.ShapeDtypeStruct((bsz, t, N_HEADS), F32),
        jax.ShapeDtypeStruct((bsz, t, N_HEADS), F32),
        jax.ShapeDtypeStruct((bsz, t, D_ATT), F32),
    ]
    ospec = lambda s: pl.BlockSpec((None, tm, s.shape[2]), row)
    return pl.pallas_call(
        _inproj_kernel,
        grid=(bsz, t // tm),
        in_specs=[pl.BlockSpec((None, tm, d), row), mod_spec, mod_spec, full(g1), full(w),
                  full(wfh), full(wfl), full(bfp), full(bglu), full(qg), full(kg),
                  full(ones_bd), full(tri)],
        out_specs=[ospec(s) for s in outs],
        out_shape=outs,
        scratch_shapes=[pltpu.VMEM((1, LANES), F32)],
        compiler_params=_cp(("parallel", "arbitrary")),
        name="inproj",
    )(x, shift, scale, g1, w, wfh, wfl, bfp, bglu, qg, kg, ones_bd, tri)


def _attn_kernel(q_ref, k_ref, v_ref, f_ref, o_ref, acc_ref, *, tq, tk):
    qi = pl.program_id(2)
    q = q_ref[...]
    lane = lax.broadcasted_iota(I32, (1, LANES), 1)
    zero = jnp.zeros_like(q)
    qm = [jnp.where(lane < HEAD_DIM, q, zero), jnp.where(lane >= HEAD_DIM, q, zero)]
    acc_ref[...] = jnp.zeros_like(acc_ref)
    rows = lax.broadcasted_iota(I32, (tq, tk), 0)
    cols = lax.broadcasted_iota(I32, (tq, tk), 1)

    def step(j, carry, diag_off=None):
        start = pl.multiple_of(j * tk, tk)
        kc = k_ref[pl.ds(start, tk), :]
        vc = v_ref[pl.ds(start, tk), :]
        fr = f_ref[j]
        new = []
        for hh in range(2):
            m, l = carry[2 * hh], carry[2 * hh + 1]
            s = _dot_nt(qm[hh], kc) - fr[hh:hh + 1, :]
            if diag_off is not None:
                s = jnp.where(cols + diag_off <= rows, s, NEG)
            m_new = jnp.maximum(m, jnp.max(s, axis=1, keepdims=True))
            alpha = jnp.exp(m - m_new)
            p = jnp.exp(s - m_new)
            l_new = alpha * l + jnp.sum(p, axis=1, keepdims=True)
            acc_ref[hh] = alpha * acc_ref[hh] + _dot(p.astype(BF16), vc)
            new += [m_new, l_new]
        return tuple(new)

    init = tuple(jnp.full((tq, 1), NEG, F32) if i % 2 == 0 else jnp.zeros((tq, 1), F32)
                 for i in range(4))
    n_diag = tq // tk
    carry = lax.fori_loop(0, qi * n_diag, step, init)
    for dd in range(n_diag):
        carry = step(qi * n_diag + dd, carry, diag_off=dd * tk)
    o0 = acc_ref[0] / carry[1]
    o1 = acc_ref[1] / carry[3]
    o_ref[...] = jnp.where(lane < HEAD_DIM, o0, o1).astype(o_ref.dtype)


def _prompt_attention(qb, kb, vb, fcum, tq=1024, tk=256):
    bsz, t, _ = qb.shape
    n_pairs = N_HEADS // 2
    nk = t // tk
    f_rows = fcum.transpose(0, 2, 1).reshape(bsz, n_pairs, 2, nk, tk).transpose(0, 1, 3, 2, 4)
    return pl.pallas_call(
        functools.partial(_attn_kernel, tq=tq, tk=tk),
        grid=(bsz, n_pairs, t // tq),
        in_specs=[pl.BlockSpec((None, tq, LANES), lambda b, p, i: (b, i, p)),
                  pl.BlockSpec((None, t, LANES), lambda b, p, i: (b, 0, p)),
                  pl.BlockSpec((None, t, LANES), lambda b, p, i: (b, 0, p)),
                  pl.BlockSpec((None, None, nk, 2, tk), lambda b, p, i: (b, p, 0, 0, 0))],
        out_specs=pl.BlockSpec((None, tq, LANES), lambda b, p, i: (b, i, p)),
        out_shape=jax.ShapeDtypeStruct((bsz, t, D_ATT), BF16),
        scratch_shapes=[pltpu.VMEM((2, tq, LANES), F32)],
        compiler_params=_cp(("parallel", "parallel", "arbitrary")),
        name="prompt_attn",
    )(qb, kb, vb, f_rows)


def _sattn_kernel(pt_ref, q_ref, *refs, n_new, n_pg):
    ck = refs[0:n_pg]
    cv = refs[n_pg:2 * n_pg]
    clf = refs[2 * n_pg:3 * n_pg]
    kn_ref, vn_ref, lfn_ref, triu_ref, o_ref, m_ref, l_ref, acc_ref, fcar_ref = refs[3 * n_pg:]
    p = pl.program_id(1)
    nq = n_new * N_HEADS
    lane = lax.broadcasted_iota(I32, (N_HEADS, D_ATT), 1)
    hrow = lax.broadcasted_iota(I32, (N_HEADS, D_ATT), 0)
    hmask = (lane // HEAD_DIM == hrow)

    @pl.when(p == 0)
    def _():
        m_ref[...] = jnp.full_like(m_ref, NEG)
        l_ref[...] = jnp.zeros_like(l_ref)
        acc_ref[...] = jnp.zeros_like(acc_ref)
        fcar_ref[...] = jnp.zeros_like(fcar_ref)

    q = q_ref[...]
    zq = jnp.zeros((N_HEADS, D_ATT), F32)
    qexp = jnp.concatenate(
        [jnp.where(hmask, jnp.broadcast_to(q[t:t + 1, :], (N_HEADS, D_ATT)), zq)
         for t in range(n_new)], axis=0).astype(BF16)

    def update(scores, values, transposed):
        m = m_ref[...]
        m_new = m
        for s in scores:
            m_new = jnp.maximum(m_new, jnp.max(s, axis=1, keepdims=True))
        alpha = jnp.exp(m - m_new)
        l_new = alpha * l_ref[...]
        acc = alpha * acc_ref[...]
        for s, vmat in zip(scores, values):
            pr = jnp.exp(s - m_new)
            l_new = l_new + jnp.sum(pr, axis=1, keepdims=True)
            pb = pr.astype(BF16)
            acc = acc + (_dot_nt(pb, vmat) if transposed else _dot(pb, vmat))
        l_ref[...] = l_new
        acc_ref[...] = acc
        m_ref[...] = m_new

    l1, l2, l3 = _split3(jnp.concatenate([r[...] for r in clf], axis=0))
    tri = triu_ref[...]
    local = _dot(l1, tri) + (_dot(l2, tri) + _dot(l3, tri))
    fcar = fcar_ref[...]
    scores = []
    for g in range(n_pg):
        fp = local[g * N_HEADS:(g + 1) * N_HEADS, :] + fcar
        fcar = jnp.broadcast_to(fp[:, PAGE_SIZE - 1:PAGE_SIZE], fcar.shape)
        scores.append(_dot(qexp, ck[g][...].astype(BF16))
                      - jnp.concatenate([fp] * n_new, axis=0))
    fcar_ref[...] = fcar
    update(scores, [cv[g][...].astype(BF16) for g in range(n_pg)], transposed=True)

    @pl.when(p == pl.num_programs(1) - 1)
    def _():
        lfn = lfn_ref[...]
        cols = []
        run = fcar[:, 0:1]
        for t in range(n_new):
            run = run + lfn[:, t:t + 1]
            cols.append(run)
        pad = SUBLANES - n_new
        fnew = jnp.concatenate(cols + [jnp.zeros((N_HEADS, pad), F32)], axis=1)
        zk = jnp.zeros((pad, D_ATT), F32)
        kn = jnp.concatenate([kn_ref[...], zk], axis=0).astype(BF16)
        vn = jnp.concatenate([vn_ref[...], zk], axis=0).astype(BF16)
        s2 = _dot_nt(qexp, kn) - jnp.concatenate([fnew] * n_new, axis=0)
        r = lax.broadcasted_iota(I32, (nq, SUBLANES), 0) // N_HEADS
        c = lax.broadcasted_iota(I32, (nq, SUBLANES), 1)
        update([jnp.where(c <= r, s2, NEG)], [vn], transposed=False)
        out = acc_ref[...] / l_ref[...]
        zo = jnp.zeros((N_HEADS, D_ATT), F32)
        rows = [jnp.sum(jnp.where(hmask, out[t * N_HEADS:(t + 1) * N_HEADS, :], zo),
                        axis=0, keepdims=True) for t in range(n_new)]
        o_ref[...] = jnp.concatenate(rows, axis=0)


def _sample_attention(qb, kn, vn, lfn, cache_k, cache_v, cache_logf, page_table, n_pg=8):
    n, s_new, _ = qb.shape
    n_pages = page_table.shape[1]
    n_phys = cache_k.shape[0]
    nq = s_new * N_HEADS
    ck = cache_k.transpose(0, 2, 3, 1).reshape(n_phys, D_ATT, PAGE_SIZE)
    cv = cache_v.transpose(0, 2, 3, 1).reshape(n_phys, D_ATT, PAGE_SIZE)
    clf = cache_logf.transpose(0, 2, 1)
    lfn_t = lfn.transpose(0, 2, 1)
    triu = jnp.triu(jnp.ones((PAGE_SIZE, PAGE_SIZE), BF16))
    seq = lambda i, p, pt: (i, 0, 0)
    tok_spec = pl.BlockSpec((None, s_new, D_ATT), seq)

    def page(g):
        return lambda i, p, pt: (pt[i * n_pages + p * n_pg + g], 0, 0)

    kv_specs = [pl.BlockSpec((None, D_ATT, PAGE_SIZE), page(g)) for g in range(n_pg)]
    lf_specs = [pl.BlockSpec((None, N_HEADS, PAGE_SIZE), page(g)) for g in range(n_pg)]
    gs = pltpu.PrefetchScalarGridSpec(
        num_scalar_prefetch=1,
        grid=(n, n_pages // n_pg),
        in_specs=[tok_spec] + kv_specs + kv_specs + lf_specs + [
            tok_spec, tok_spec,
            pl.BlockSpec((None, N_HEADS, s_new), seq),
            pl.BlockSpec(triu.shape, lambda i, p, pt: (0, 0))],
        out_specs=tok_spec,
        scratch_shapes=[pltpu.VMEM((nq, 1), F32), pltpu.VMEM((nq, 1), F32),
                        pltpu.VMEM((nq, D_ATT), F32),
                        pltpu.VMEM((N_HEADS, PAGE_SIZE), F32)],
    )
    return pl.pallas_call(
        functools.partial(_sattn_kernel, n_new=s_new, n_pg=n_pg),
        grid_spec=gs,
        out_shape=jax.ShapeDtypeStruct((n, s_new, D_ATT), F32),
        compiler_params=_cp(("parallel", "arbitrary")),
        name="sample_attn",
    )(page_table.reshape(-1), qb, *([ck] * n_pg), *([cv] * n_pg), *([clf] * n_pg),
      kn, vn, lfn_t, triu)


def _ln_swish(y, g, b):
    mu = jnp.mean(y, axis=-1, keepdims=True)
    yc = y - mu
    var = jnp.mean(yc * yc, axis=-1, keepdims=True)
    yn = yc * lax.rsqrt(var + EPS) * g + b
    return yn * jax.nn.sigmoid(yn)


def _conv_kernel(u_ref, cw_ref, cb_ref, g_ref, b_ref, o_ref, ext_ref, *, tm, hist):
    t = pl.program_id(1)

    @pl.when(t == 0)
    def _():
        ext_ref[0:hist, :] = jnp.zeros((hist, ext_ref.shape[1]), F32)

    ext_ref[hist:hist + tm, :] = u_ref[...]
    off = hist - (CONV_WIDTH - 1)
    y = jnp.zeros(u_ref.shape, F32) + cb_ref[...]
    for w in range(CONV_WIDTH):
        y = y + ext_ref[off + w:off + w + tm, :] * cw_ref[w:w + 1, :]
    o_ref[...] = _ln_swish(y, g_ref[...], b_ref[...]).astype(o_ref.dtype)
    ext_ref[0:hist, :] = ext_ref[tm:tm + hist, :]


def _conv_prompt(u, cw, cb, g, b, tm=512):
    bsz, t, c = u.shape
    hist = 32
    row = lambda bb, i: (bb, i, 0)
    const = lambda bb, i: (0, 0)
    return pl.pallas_call(
        functools.partial(_conv_kernel, tm=tm, hist=hist),
        grid=(bsz, t // tm),
        in_specs=[pl.BlockSpec((None, tm, c), row), pl.BlockSpec(cw.shape, const),
                  pl.BlockSpec((1, c), const), pl.BlockSpec((1, c), const),
                  pl.BlockSpec((1, c), const)],
        out_specs=pl.BlockSpec((None, tm, c), row),
        out_shape=jax.ShapeDtypeStruct((bsz, t, c), BF16),
        scratch_shapes=[pltpu.VMEM((hist + tm, c), F32)],
        compiler_params=_cp(("parallel", "arbitrary")),
        name="conv_prompt",
    )(u, cw, cb, g, b)


def _conv_step_kernel(st_ref, u_ref, cw_ref, cb_ref, g_ref, b_ref, o_ref, *, n_hist, n_new):
    def ext(i):
        return st_ref[i] if i < n_hist else u_ref[i - n_hist]

    for t in range(n_new):
        y = jnp.zeros(o_ref.shape[1:], F32) + cb_ref[...]
        for w in range(CONV_WIDTH):
            y = y + ext(t + w) * cw_ref[w:w + 1, :]
        o_ref[t] = _ln_swish(y, g_ref[...], b_ref[...]).astype(o_ref.dtype)


def _conv_sample(state_t, u_t, cw, cb, g, b):
    n_hist, n, c = state_t.shape
    n_new = u_t.shape[0]
    return pl.pallas_call(
        functools.partial(_conv_step_kernel, n_hist=n_hist, n_new=n_new),
        out_shape=jax.ShapeDtypeStruct((n_new, n, c), BF16),
        compiler_params=pltpu.CompilerParams(vmem_limit_bytes=VMEM_LIMIT),
        name="conv_sample",
    )(state_t, u_t, cw, cb, g, b)


def _tail_kernel(x_ref, a_ref, c_ref, g1_ref, sh_ref, sc_ref, n2_ref, woa_ref, woc_ref,
                 x1_ref, h2_ref):
    proj = (_dot(a_ref[...].astype(BF16), woa_ref[...])
            + _dot(c_ref[...].astype(BF16), woc_ref[...]))
    x1 = x_ref[...] + g1_ref[...] * proj
    x1_ref[...] = x1
    ms = jnp.mean(x1 * x1, axis=-1, keepdims=True)
    h2 = x1 * lax.rsqrt(ms + EPS) * n2_ref[...]
    h2_ref[...] = h2 * (1.0 + sc_ref[...]) + sh_ref[...]


def _tail(x, attn, conv, gate1, shift2, scale2, n2g, woa, woc, per_token, tm=256):
    bsz, t, d = x.shape
    row = lambda b, i: (b, i, 0)
    const2 = lambda b, i: (0, 0)
    mod_spec = (pl.BlockSpec((None, tm, d), row) if per_token
                else pl.BlockSpec((None, 1, d), lambda b, i: (b, 0, 0)))
    half = pl.BlockSpec((None, tm, D_ATT), row)
    full = pl.BlockSpec((None, tm, d), row)
    return pl.pallas_call(
        _tail_kernel,
        grid=(bsz, t // tm),
        in_specs=[full, half, half, mod_spec, mod_spec, mod_spec,
                  pl.BlockSpec(n2g.shape, const2), pl.BlockSpec(woa.shape, const2),
                  pl.BlockSpec(woc.shape, const2)],
        out_specs=[full, full],
        out_shape=[jax.ShapeDtypeStruct((bsz, t, d), F32)] * 2,
        compiler_params=_cp(("parallel", "parallel")),
        name="tail",
    )(x, attn, conv, gate1, shift2, scale2, n2g, woa, woc)


def _top_rows(s, order, payload, k):
    big = float(2 ** 24)
    vals, outs = [], []
    for _ in range(k):
        m = jnp.max(s, axis=0, keepdims=True)
        pos = jnp.min(jnp.where(s == m, order, big), axis=0, keepdims=True)
        hit = order == pos
        vals.append(m)
        if payload is None:
            outs.append(pos)
        else:
            outs.append(jnp.sum(jnp.where(hit, payload, 0), axis=0, keepdims=True))
        s = jnp.where(hit, -jnp.inf, s)
    return jnp.concatenate(vals, axis=0), jnp.concatenate(outs, axis=0)


def _pair_groups():
    groups = [(0, PEER_TOPK, PEER_TOPK)]
    for a in range(1, SUBLANES):
        groups.append((a, SUBLANES, PEER_TOPK // (a + 1)))
    groups.append((None, SUBLANES, SUBLANES))
    return groups


def _route_kernel(h_ref, wq_ref, skh_ref, skl_ref, e_ref, g_ref):
    hb = h_ref[...].astype(BF16)
    tb = hb.shape[0]
    key_order = lax.broadcasted_iota(I32, (PEER_N_KEYS, tb), 0).astype(F32)
    groups = _pair_groups()
    orders, masks = [], []
    for a, rows, valid in groups:
        r = lax.broadcasted_iota(I32, (rows, tb), 0)
        flat = (a * PEER_TOPK + r) if a is not None else (SUBLANES + r) * PEER_TOPK
        orders.append(flat.astype(F32))
        masks.append(r < valid)
    pair_order = jnp.concatenate(orders, axis=0)
    pair_valid = jnp.concatenate(masks, axis=0)
    e_rows, g_rows = [], []
    for hd in range(PEER_HEADS):
        tops = []
        for half in range(2):
            i = hd * 2 + half
            wq = wq_ref[i * PEER_HALF:(i + 1) * PEER_HALF, :]
            qt = _dot_nt(wq, hb)
            qh, ql = _split2(qt)
            skh = skh_ref[i]
            s = _dot(skh, qh) + (_dot(skh, ql) + _dot(skl_ref[i], qh))
            vals, pos = _top_rows(s, key_order, None, PEER_TOPK)
            tops.append((vals, pos.astype(I32)))
        (s1, i1), (s2, i2) = tops
        cand, cidx = [], []
        for a, rows, _ in groups:
            if a is not None:
                cand.append(s1[a:a + 1, :] + s2[0:rows, :])
                cidx.append(i1[a:a + 1, :] * PEER_N_KEYS + i2[0:rows, :])
            else:
                cand.append(s1[SUBLANES:, :] + s2[0:1, :])
                cidx.append(i1[SUBLANES:, :] * PEER_N_KEYS + i2[0:1, :])
        cand = jnp.where(pair_valid, jnp.concatenate(cand, axis=0), -jnp.inf)
        top, eidx = _top_rows(cand, pair_order, jnp.concatenate(cidx, axis=0), PEER_TOPK)
        ex = jnp.exp(top - top[0:1, :])
        g_rows.append(ex / jnp.sum(ex, axis=0, keepdims=True))
        e_rows.append(eidx)
    e_ref[...] = jnp.concatenate(e_rows, axis=0).T
    g_ref[...] = jnp.concatenate(g_rows, axis=0).T


def _route(h2, wq_t, skh, skl, tb=128):
    n, d = h2.shape
    const2 = lambda i: (0, 0)
    const3 = lambda i: (0, 0, 0)
    return pl.pallas_call(
        _route_kernel,
        grid=(n // tb,),
        in_specs=[pl.BlockSpec((tb, d), lambda i: (i, 0)), pl.BlockSpec(wq_t.shape, const2),
                  pl.BlockSpec(skh.shape, const3), pl.BlockSpec(skl.shape, const3)],
        out_specs=[pl.BlockSpec((tb, PEER_SLOTS), lambda i: (i, 0))] * 2,
        out_shape=[jax.ShapeDtypeStruct((n, PEER_SLOTS), I32),
                   jax.ShapeDtypeStruct((n, PEER_SLOTS), F32)],
        compiler_params=_cp(("parallel",)),
        name="peer_route",
    )(h2, wq_t, skh, skl)


def _peer_act_kernel(idx_ref, tab_ref, h_ref, g_ref, c_ref, ps0_ref, ps1_ref, ps2_ref, ps3_ref,
                     act_ref, *, tb):
    def gather(t, ps_ref):
        hv = h_ref[t]
        idx_t = idx_ref.at[pl.ds(t * PEER_SLOTS, PEER_SLOTS)]
        for j in range(PEER_SLOTS):
            ps_ref[j * SUBLANES:(j + 1) * SUBLANES, :] = tab_ref[idx_t[j]].astype(F32) * hv

    def reduce(t, ps_ref):
        s = ps_ref[pl.ds(0, PEER_SLOTS, stride=SUBLANES), :]
        for r in range(1, SUBLANES):
            s = s + ps_ref[pl.ds(r, PEER_SLOTS, stride=SUBLANES), :]
        act_ref[pl.ds(t, 1), :] = jnp.sum(s.T, axis=0, keepdims=True)

    ps = (ps0_ref, ps1_ref, ps2_ref, ps3_ref)

    per_trip = 4

    def trip(i, carry):
        for k in range(per_trip):
            gather(per_trip * i + k, ps[k % 4])
            reduce(jnp.maximum(per_trip * i + k - 2, 0), ps[(k + 2) % 4])
        return carry

    ps2_ref[...] = jnp.zeros_like(ps2_ref)
    ps3_ref[...] = jnp.zeros_like(ps3_ref)
    lax.fori_loop(0, tb // per_trip, trip, 0)
    reduce(tb - 2, ps2_ref)
    reduce(tb - 1, ps3_ref)
    act = act_ref[...]
    gelu = 0.5 * act * (1.0 + lax.erf(act * (2.0 ** -0.5)))
    c_ref[...] = g_ref[...] * gelu


def _peer_act(eidx, tab_u, h2, g, tb=64):
    n = eidx.shape[0]
    d = h2.shape[1]
    hc = h2.reshape(n, d // LANES, LANES)
    return pl.pallas_call(
        functools.partial(_peer_act_kernel, tb=tb),
        grid=(n // tb,),
        in_specs=[pl.BlockSpec((tb * PEER_SLOTS,), lambda i: (i,), memory_space=pltpu.SMEM),
                  pl.BlockSpec(memory_space=pltpu.VMEM),
                  pl.BlockSpec((tb, d // LANES, LANES), lambda i: (i, 0, 0)),
                  pl.BlockSpec((tb, PEER_SLOTS), lambda i: (i, 0))],
        out_specs=pl.BlockSpec((tb, PEER_SLOTS), lambda i: (i, 0)),
        out_shape=jax.ShapeDtypeStruct((n, PEER_SLOTS), F32),
        scratch_shapes=[pltpu.VMEM((PEER_SLOTS * SUBLANES, LANES), F32)] * 4
        + [pltpu.VMEM((tb, PEER_SLOTS), F32)],
        compiler_params=_cp(("arbitrary",)),
        name="peer_act",
    )(eidx.reshape(-1), tab_u, hc, g)


def _peer_out_kernel(idx_ref, tab_ref, c_ref, o_ref, cb0_ref, cb1_ref, *, tb):
    n_acc = 4

    def prepare(t, cb_ref):
        crow = c_ref[pl.ds(t, 1), :]
        cb_ref[...] = jnp.broadcast_to(crow, (PEER_SLOTS, PEER_SLOTS)).T

    def tok(t, cb_ref):
        idx_t = idx_ref.at[pl.ds(t * PEER_SLOTS, PEER_SLOTS)]
        accs = [jnp.zeros((SUBLANES, LANES), F32) for _ in range(n_acc)]
        for j in range(PEER_SLOTS):
            accs[j % n_acc] = (accs[j % n_acc]
                               + cb_ref[j:j + 1, :] * tab_ref[idx_t[j]].astype(F32))
        o_ref[t] = (accs[0] + accs[1]) + (accs[2] + accs[3])

    def pair(i, carry):
        prepare(2 * i + 1, cb1_ref)
        tok(2 * i, cb0_ref)
        prepare(jnp.minimum(2 * i + 2, tb - 1), cb0_ref)
        tok(2 * i + 1, cb1_ref)
        return carry

    prepare(0, cb0_ref)
    lax.fori_loop(0, tb // 2, pair, 0)


def _peer_out(eidx, coef, tab_v, tb=64):
    n = eidx.shape[0]
    chunks = tab_v.shape[1]
    out = pl.pallas_call(
        functools.partial(_peer_out_kernel, tb=tb),
        grid=(n // tb,),
        in_specs=[pl.BlockSpec((tb * PEER_SLOTS,), lambda i: (i,), memory_space=pltpu.SMEM),
                  pl.BlockSpec(memory_space=pltpu.VMEM),
                  pl.BlockSpec((tb, PEER_SLOTS), lambda i: (i, 0))],
        out_specs=pl.BlockSpec((tb, chunks, LANES), lambda i: (i, 0, 0)),
        out_shape=jax.ShapeDtypeStruct((n, chunks, LANES), F32),
        scratch_shapes=[pltpu.VMEM((PEER_SLOTS, PEER_SLOTS), F32)] * 2,
        compiler_params=_cp(("arbitrary",)),
        name="peer_out",
    )(eidx.reshape(-1), tab_v, coef)
    return out.reshape(n, chunks * LANES)


def _final_kernel(x1_ref, g2_ref, p_ref, y_ref):
    y_ref[...] = x1_ref[...] + g2_ref[...] * p_ref[...]


def _final(x1, gate2, peer, per_token, tm=512):
    bsz, t, d = x1.shape
    row = lambda b, i: (b, i, 0)
    mod_spec = (pl.BlockSpec((None, tm, d), row) if per_token
                else pl.BlockSpec((None, 1, d), lambda b, i: (b, 0, 0)))
    full = pl.BlockSpec((None, tm, d), row)
    return pl.pallas_call(
        _final_kernel,
        grid=(bsz, t // tm),
        in_specs=[full, mod_spec, full],
        out_specs=full,
        out_shape=jax.ShapeDtypeStruct((bsz, t, d), F32),
        compiler_params=_cp(("parallel", "parallel")),
        name="final",
    )(x1, gate2, peer)


def _prep_inproj(w_in, b_f, b_glu, q_norm_g, k_norm_g, tm):
    d = D_ATT
    nf = 3 * d + N_HEADS
    w = jnp.concatenate([w_in[:, :3 * d], w_in[:, nf:]], axis=1).astype(BF16)
    wf = jnp.pad(w_in[:, 3 * d:nf], ((0, 0), (0, LANES - N_HEADS)))
    wfh = wf.astype(BF16)
    wfl = (wf - wfh.astype(F32)).astype(BF16)
    bfp = jnp.pad(b_f, (0, LANES - N_HEADS)).reshape(1, LANES)
    qg = jnp.tile(q_norm_g, N_HEADS).reshape(1, d)
    kg = jnp.tile(k_norm_g, N_HEADS).reshape(1, d)
    hid = jnp.arange(d) // HEAD_DIM
    ones_bd = (hid[:, None] == hid[None, :]).astype(BF16)
    tri = jnp.tril(jnp.ones((tm, tm), BF16))
    return (w, wfh, wfl, bfp, b_glu.reshape(1, -1), qg, kg, ones_bd, tri)


def _peer_ffn(h2, wq_t, skh, skl, tab_u, tab_v):
    eidx, g = _route(h2, wq_t, skh, skl)
    coef = _peer_act(eidx, tab_u, h2, g)
    return _peer_out(eidx, coef, tab_v)


def kernel(x_prompt, x_sample, c_prompt, c_sample, cache_k, cache_v, cache_logf, state_conv,
           page_table, norm1_g, norm2_g, w_ada, b_ada, w_in, b_f, b_glu, q_norm_g, k_norm_g,
           conv_w, conv_b, conv_ln_g, conv_ln_b, w_o, peer_w_query, peer_sub_keys, peer_u,
           peer_v):
    depth = w_ada.shape[0]
    bsz, seq, d = x_prompt.shape
    n_dec, s_new, _ = x_sample.shape
    tm = 256
    xp, xs = x_prompt, x_sample.reshape(1, n_dec * s_new, d)
    outs = [[] for _ in range(8)]
    for l in range(depth):
        n_c = bsz + n_dec
        pad = (-n_c) % SUBLANES
        c_all = jnp.pad(jnp.concatenate([c_prompt, c_sample], axis=0), ((0, pad), (0, 0)))
        mod = _adaln(c_all, w_ada[l], b_ada[l])
        mod_p = [m.reshape(bsz, 1, d) for m in jnp.split(mod[:bsz], 6, axis=-1)]
        mod_s = [jnp.repeat(m, s_new, axis=0).reshape(1, n_dec * s_new, d)
                 for m in jnp.split(mod[bsz:n_c], 6, axis=-1)]

        wts = _prep_inproj(w_in[l], b_f[l], b_glu[l], q_norm_g[l], k_norm_g[l], tm)
        g1 = norm1_g[l].reshape(1, d)
        n2 = norm2_g[l].reshape(1, d)
        cw = conv_w[l]
        cb, lg, lb = (a[l].reshape(1, -1) for a in (conv_b, conv_ln_g, conv_ln_b))
        woa = w_o[l][:D_ATT].astype(BF16)
        woc = w_o[l][D_ATT:].astype(BF16)
        wq_t = peer_w_query[l].T.astype(BF16)
        sk = peer_sub_keys[l].reshape(PEER_HEADS * 2, PEER_N_KEYS, PEER_HALF)
        skh = sk.astype(BF16)
        skl = (sk - skh.astype(F32)).astype(BF16)
        tab_u = peer_u[l].astype(BF16).reshape(-1, d // LANES, LANES)
        tab_v = peer_v[l].astype(BF16).reshape(-1, d // LANES, LANES)

        sh1, sc1, gt1, sh2, sc2, gt2 = mod_p
        qb, k, v, kb, vb, lf, fc, u = _inproj(xp, sh1, sc1, g1, wts, per_token=False, tm=tm)
        a_out = _prompt_attention(qb, kb, vb, fc)
        c_out = _conv_prompt(u, cw, cb, lg, lb)
        x1, h2 = _tail(xp, a_out, c_out, gt1, sh2, sc2, n2, woa, woc, per_token=False, tm=tm)
        peer = _peer_ffn(h2.reshape(bsz * seq, d), wq_t, skh, skl, tab_u, tab_v)
        xp = _final(x1, gt2, peer.reshape(bsz, seq, d), per_token=False)
        outs[0].append(k.reshape(bsz, seq, N_HEADS, HEAD_DIM))
        outs[1].append(v.reshape(bsz, seq, N_HEADS, HEAD_DIM))
        outs[2].append(lf)
        outs[3].append(u[:, seq - (CONV_WIDTH - 1):, :])

        sh1, sc1, gt1, sh2, sc2, gt2 = mod_s
        qb, k, v, kb, vb, lf, fc, u = _inproj(xs, sh1, sc1, g1, wts, per_token=True, tm=tm)
        shp = (n_dec, s_new, D_ATT)
        a_out = _sample_attention(qb.reshape(shp).astype(F32), k.reshape(shp), v.reshape(shp),
                                  lf.reshape(n_dec, s_new, N_HEADS), cache_k[l], cache_v[l],
                                  cache_logf[l], page_table)
        u_s = u.reshape(shp)
        c_out = _conv_sample(state_conv[l].transpose(1, 0, 2), u_s.transpose(1, 0, 2),
                             cw, cb, lg, lb).transpose(1, 0, 2)
        x1, h2 = _tail(xs, a_out.reshape(1, -1, D_ATT), c_out.reshape(1, -1, D_ATT), gt1, sh2,
                       sc2, n2, woa, woc, per_token=True, tm=tm)
        peer = _peer_ffn(h2.reshape(n_dec * s_new, d), wq_t, skh, skl, tab_u, tab_v)
        xs = _final(x1, gt2, peer.reshape(1, n_dec * s_new, d), per_token=True)
        outs[4].append(k.reshape(n_dec, s_new, N_HEADS, HEAD_DIM))
        outs[5].append(v.reshape(n_dec, s_new, N_HEADS, HEAD_DIM))
        outs[6].append(lf.reshape(n_dec, s_new, N_HEADS))
        outs[7].append(jnp.concatenate([state_conv[l], u_s], axis=1)[:, -(CONV_WIDTH - 1):])
    st = [jnp.stack(o) for o in outs]
    return (xp, xs.reshape(n_dec, s_new, d), st[0], st[1], st[2], st[3], st[4], st[5], st[6],
            st[7])
```
